```python
import math
import jax, jax.numpy as jnp
from jax import lax
import numpy as np

D_MODEL = 4096
BATCH = 16
SEQ = 256
DEPTH = 4
DEC_BATCH = 4
DEC_SEQ = 1024
PAST_LEN = 512

GRID_W = 64
N_MIXERS = 3
RET_HEADS = 16
RET_DK = D_MODEL // RET_HEADS
RET_DV = 2 * RET_DK
RET_CHUNK = 128
S5_GROUP = 16
S5_GROUPS = D_MODEL // S5_GROUP
S5_STATE = 64
DA_HEADS = 16
DA_DH = D_MODEL // (2 * DA_HEADS)
DA_VD = 2 * DA_DH
Q_BLOCK = 128
ROPE_BASE = 10000.0
N_EXPERTS = 16
N_EXPERT_GROUPS = 4
EXPERTS_PER_GROUP = N_EXPERTS // N_EXPERT_GROUPS
TOP_K = 2
D_EXPERT = D_MODEL // 4
MOE_BLOCK = 128
EPS = 1e-6

kernel_name = "hybrid_prefix_diffusion_step"


def rms_norm(x, g):
    xf = x.astype(jnp.float32)
    y = xf * lax.rsqrt(jnp.mean(xf * xf, axis=-1, keepdims=True) + EPS)
    return y.astype(x.dtype) * g


def modulation(cvec, w, b):
    m = jax.nn.silu(cvec) @ w + b
    m = m.reshape(m.shape[0], 6, 1, D_MODEL)
    return [m[:, j] for j in range(6)]


def retention_scan(q, k, v, log_gamma, s0):
    b, L, h, _ = q.shape
    n = L // RET_CHUNK
    idx = jnp.arange(RET_CHUNK, dtype=jnp.float32)
    rel = idx[:, None] - idx[None, :]
    intra = jnp.where(rel >= 0, jnp.exp(log_gamma[:, None, None] * jnp.maximum(rel, 0.0)), 0.0)
    q_decay = jnp.exp(log_gamma[None, :] * (idx[:, None] + 1.0))[..., None]
    k_decay = jnp.exp(log_gamma[None, :] * (RET_CHUNK - 1.0 - idx[:, None]))[..., None]
    chunk_decay = jnp.exp(log_gamma * RET_CHUNK)[:, None, None]

    def chunks(t):
        return t.reshape(b, n, RET_CHUNK, h, t.shape[-1]).swapaxes(0, 1)

    def step(s, inp):
        qc, kc, vc = inp
        att = jnp.einsum("bihd,bjhd->bhij", qc, kc) * intra
        o = jnp.einsum("bhij,bjhe->bihe", att, vc) + jnp.einsum("bihd,bhde->bihe", qc, s) * q_decay
        s = s * chunk_decay + jnp.einsum("bjhd,bjhe->bhde", kc * k_decay, vc)
        return s, o

    s, o = lax.scan(step, s0, (chunks(q), chunks(k), chunks(v)))
    return o.swapaxes(0, 1).reshape(b, L, h, -1), s


def retention_mixer(h, w_in, w_o, gn, decay, s0):
    b, L, _ = h.shape
    f32 = jnp.float32
    qk = RET_HEADS * RET_DK
    q, k, v, g = jnp.split(h @ w_in, [qk, 2 * qk, 2 * qk + RET_HEADS * RET_DV], axis=-1)
    q = q.reshape(b, L, RET_HEADS, RET_DK).astype(f32)
    k = k.reshape(b, L, RET_HEADS, RET_DK).astype(f32) * (RET_DK ** -0.5)
    v = v.reshape(b, L, RET_HEADS, RET_DV).astype(f32)
    s0 = s0.astype(f32)
    log_gamma = -jnp.exp(decay.astype(f32))
    o_f, s_f = retention_scan(q, k, v, log_gamma[0], s0[:, 0])
    o_b, s_b = retention_scan(q[:, ::-1], k[:, ::-1], v[:, ::-1], log_gamma[1], s0[:, 1])
    o = o_f + o_b[:, ::-1]
    mu = jnp.mean(o, axis=-1, keepdims=True)
    var = jnp.mean(jnp.square(o - mu), axis=-1, keepdims=True)
    o = ((o - mu) * lax.rsqrt(var + EPS)).reshape(b, L, RET_HEADS * RET_DV).astype(h.dtype) * gn
    y = (jax.nn.silu(g) * o) @ w_o
    return y, jnp.stack([s_f, s_b], axis=1)


def _complex_affine_combine(e1, e2):
    a1r, a1i, b1r, b1i = e1
    a2r, a2i, b2r, b2i = e2
    return (a2r * a1r - a2i * a1i,
            a2r * a1i + a2i * a1r,
            a2r * b1r - a2i * b1i + b2r,
            a2r * b1i + a2i * b1r + b2i)


def s5_mixer(u, lam_re, lam_im, log_dt, b_re, b_im, c_re, c_im, d_skip, w_glu, b_glu, h0):
    b, L, _ = u.shape
    f32 = jnp.float32
    uf = u.astype(f32).reshape(b, L, S5_GROUPS, S5_GROUP).swapaxes(0, 1)
    lam_re = lam_re.astype(f32)
    lam_im = lam_im.astype(f32)
    dt = jnp.exp(log_dt.astype(f32))[..., None]
    dl_re, dl_im = lam_re * dt, lam_im * dt
    mag = jnp.exp(dl_re)
    a_re, a_im = mag * jnp.cos(dl_im), mag * jnp.sin(dl_im)
    den = lam_re * lam_re + lam_im * lam_im
    z_re = a_re - 1.0
    coef_re = (z_re * lam_re + a_im * lam_im) / den
    coef_im = (a_im * lam_re - z_re * lam_im) / den
    b_re = b_re.astype(f32)
    b_im = b_im.astype(f32)
    bb_re = coef_re[..., None] * b_re - coef_im[..., None] * b_im
    bb_im = coef_re[..., None] * b_im + coef_im[..., None] * b_re
    h0 = h0.astype(f32)
    n_fwd = jnp.arange(1, L + 1, dtype=f32)
    ys, finals = [], []
    for d, reverse in ((0, False), (1, True)):
        bu_re = jnp.einsum("lbgc,gpc->lbgp", uf, bb_re[d])
        bu_im = jnp.einsum("lbgc,gpc->lbgp", uf, bb_im[d])
        ar = jnp.broadcast_to(a_re[d], (L, 1, S5_GROUPS, S5_STATE))
        ai = jnp.broadcast_to(a_im[d], (L, 1, S5_GROUPS, S5_STATE))
        _, _, hr, hi = lax.associative_scan(_complex_affine_combine, (ar, ai, bu_re, bu_im),
                                            reverse=reverse, axis=0)
        steps = n_fwd[::-1] if reverse else n_fwd
        pmag = jnp.exp(steps[:, None, None] * dl_re[d])
        pang = steps[:, None, None] * dl_im[d]
        p_re = (pmag * jnp.cos(pang))[:, None]
        p_im = (pmag * jnp.sin(pang))[:, None]
        i_re, i_im = h0[:, d, ..., 0], h0[:, d, ..., 1]
        hr = hr + p_re * i_re - p_im * i_im
        hi = hi + p_re * i_im + p_im * i_re
        ys.append(jnp.einsum("lbgp,gcp->lbgc", hr, c_re[d].astype(f32))
                  - jnp.einsum("lbgp,gcp->lbgc", hi, c_im[d].astype(f32)))
        last = 0 if reverse else L - 1
        finals.append(jnp.stack([hr[last], hi[last]], axis=-1))
    y = (ys[0] + ys[1]).swapaxes(0, 1).reshape(b, L, D_MODEL) + d_skip.astype(f32) * u.astype(f32)
    y = jax.nn.gelu(y).astype(u.dtype)
    val, gate = jnp.split(y @ w_glu + b_glu, 2, axis=-1)
    return val * jax.nn.sigmoid(gate), jnp.stack(finals, axis=1)


def axial_rope(L):
    rows = L // GRID_W
    row = jnp.repeat(jnp.arange(rows, dtype=jnp.float32), GRID_W)
    col = jnp.tile(jnp.arange(GRID_W, dtype=jnp.float32), rows)
    n_freq = DA_DH // 4
    inv = ROPE_BASE ** (-jnp.arange(n_freq, dtype=jnp.float32) / n_freq)
    ang = jnp.stack([row[:, None] * inv, col[:, None] * inv], axis=1)
    return jnp.cos(ang), jnp.sin(ang)


def apply_rope(x, cos, sin):
    xr = x.reshape(*x.shape[:-1], 2, 2, DA_DH // 4)
    x1, x2 = xr[..., 0, :], xr[..., 1, :]
    cs = cos[None, :, None, None].astype(x.dtype)
    sn = sin[None, :, None, None].astype(x.dtype)
    return jnp.stack([x1 * cs - x2 * sn, x2 * cs + x1 * sn], axis=-2).reshape(x.shape)


def diff_lambda(lam_p, lam_init):
    lp = lam_p.astype(jnp.float32)
    return jnp.exp(jnp.sum(lp[0] * lp[1])) - jnp.exp(jnp.sum(lp[2] * lp[3])) + lam_init


def diff_attn_project(h, w_qkv):
    b, L, _ = h.shape
    q, k, v = jnp.split(h @ w_qkv, 3, axis=-1)
    return (q.reshape(b, L, DA_HEADS, 2, DA_DH), k.reshape(b, L, DA_HEADS, 2, DA_DH),
            v.reshape(b, L, DA_HEADS, DA_VD))


def diff_attention(q, k, v, lam):
    b, lq = q.shape[:2]
    n_blk = lq // Q_BLOCK
    qb = q.reshape(b, n_blk, Q_BLOCK, DA_HEADS, 2, DA_DH).swapaxes(0, 1)
    scale = DA_DH ** -0.5

    def block(q_blk):
        s = jnp.einsum("bqhmd,bkhmd->bhmqk", q_blk, k, preferred_element_type=jnp.float32) * scale
        p = jax.nn.softmax(s, axis=-1)
        p = p[:, :, 0] - lam * p[:, :, 1]
        return jnp.einsum("bhqk,bkhe->bqhe", p.astype(v.dtype), v)

    o = lax.map(block, qb)
    return o.swapaxes(0, 1).reshape(b, lq, DA_HEADS, DA_VD)


def diff_attn_out(o, w_o, subln, lam_init):
    b, L = o.shape[:2]
    of = o.astype(jnp.float32)
    of = of * lax.rsqrt(jnp.mean(of * of, axis=-1, keepdims=True) + EPS) * (1.0 - lam_init)
    return (of.reshape(b, L, D_MODEL).astype(o.dtype) * subln) @ w_o


def route(x, router_w, router_b):
    n_tok = x.shape[0]
    probs = jax.nn.softmax((x @ router_w).astype(jnp.float32), axis=-1)
    sel = (probs + router_b.astype(jnp.float32)).reshape(n_tok, N_EXPERT_GROUPS, EXPERTS_PER_GROUP)
    group_score = jnp.sum(lax.top_k(sel, TOP_K)[0], axis=-1)
    grp = jnp.argmax(group_score, axis=-1)
    in_grp = sel[jnp.arange(n_tok), grp]
    _, local = lax.top_k(in_grp, TOP_K)
    ids = (grp[:, None] * EXPERTS_PER_GROUP + local).astype(jnp.int32)
    w = jnp.take_along_axis(probs, ids, axis=-1)
    return ids, w / jnp.sum(w, axis=-1, keepdims=True)


def moe_ffn(h, router_w, router_b, w13, w2):
    b, L, _ = h.shape
    x = h.reshape(b * L, D_MODEL)
    ids, gates = route(x, router_w, router_b)
    n_tok = x.shape[0]
    n_asg = n_tok * TOP_K
    flat_e = ids.reshape(-1)
    order = jnp.argsort(flat_e)
    e_s = flat_e[order]
    t_s = (order // TOP_K).astype(jnp.int32)
    g_s = gates.reshape(-1)[order]
    counts = jnp.bincount(flat_e, length=N_EXPERTS)
    padded = (counts + MOE_BLOCK - 1) // MOE_BLOCK * MOE_BLOCK
    pad_end = jnp.cumsum(padded)
    dest = (pad_end - padded)[e_s] + jnp.arange(n_asg) - (jnp.cumsum(counts) - counts)[e_s]
    n_blk = -(-n_asg // MOE_BLOCK) + N_EXPERTS
    n_rows = n_blk * MOE_BLOCK
    row_tok = jnp.full((n_rows,), n_tok, jnp.int32).at[dest].set(t_s)
    row_gate = jnp.zeros((n_rows,), jnp.float32).at[dest].set(g_s)
    blk_e = jnp.minimum(jnp.searchsorted(pad_end, jnp.arange(n_blk) * MOE_BLOCK, side="right"),
                        N_EXPERTS - 1)
    x_pad = jnp.concatenate([x, jnp.zeros((1, D_MODEL), x.dtype)], axis=0)

    def expert_block(args):
        tok, e = args
        gu = x_pad[tok] @ w13[e]
        g, u = jnp.split(gu, 2, axis=-1)
        return (jax.nn.silu(g) * u) @ w2[e]

    y = lax.map(expert_block, (row_tok.reshape(n_blk, MOE_BLOCK), blk_e)).reshape(n_rows, D_MODEL)
    out = jnp.zeros((n_tok + 1, D_MODEL), jnp.float32).at[row_tok].add(
        y.astype(jnp.float32) * row_gate[:, None])
    return out[:n_tok].astype(h.dtype).reshape(b, L, D_MODEL)


def setup_inputs(seed: int = 0) -> dict:
    key = jax.random.key(seed)
    ks = iter(jax.random.split(key, 64))
    f32 = jnp.float32

    def nrm(shape, scale=1.0):
        return jax.random.normal(next(ks), shape, f32) * scale

    def gain(shape):
        return 1.0 + nrm(shape, 0.02)

    ret_in = 2 * RET_HEADS * RET_DK + 2 * RET_HEADS * RET_DV
    ret_decay_base = jnp.log(-jnp.log1p(-jnp.exp2(-(5.0 + jnp.arange(RET_HEADS, dtype=f32)))))
    s5_n = jnp.arange(S5_STATE, dtype=f32)

    x_prompt = nrm((BATCH, SEQ, D_MODEL))
    x_sample = nrm((DEC_BATCH, DEC_SEQ, D_MODEL))
    state_ret_l0 = nrm((DEC_BATCH, 2, RET_HEADS, RET_DK, RET_DV), 0.5)
    state_s5_l1 = nrm((DEC_BATCH, 2, S5_GROUPS, S5_STATE, 2))
    cache_k_l2 = nrm((DEC_BATCH, PAST_LEN, DA_HEADS, 2, DA_DH))
    cache_v_l2 = nrm((DEC_BATCH, PAST_LEN, DA_HEADS, DA_VD))
    state_ret_l3 = nrm((DEC_BATCH, 2, RET_HEADS, RET_DK, RET_DV), 0.5)
    c = nrm((DEC_BATCH, D_MODEL))
    c_ctx = nrm((D_MODEL,))
    ada_w = nrm((DEPTH, D_MODEL, 6 * D_MODEL), 0.5 * D_MODEL ** -0.5)
    ada_b = nrm((DEPTH, 6 * D_MODEL), 0.02)
    norm_w = gain((DEPTH, 2, D_MODEL))
    final_norm_w = gain((D_MODEL,))
    router_w = nrm((D_MODEL, N_EXPERTS), D_MODEL ** -0.5)
    router_b = nrm((N_EXPERTS,), 0.01)
    moe_w13 = nrm((DEPTH, N_EXPERTS, D_MODEL, 2 * D_EXPERT), D_MODEL ** -0.5)
    moe_w2 = nrm((DEPTH, N_EXPERTS, D_EXPERT, D_MODEL), D_EXPERT ** -0.5)
    l0_ret_w_in = nrm((D_MODEL, ret_in), D_MODEL ** -0.5)
    l0_ret_w_o = nrm((RET_HEADS * RET_DV, D_MODEL), (RET_HEADS * RET_DV) ** -0.5)
    l0_ret_gn = gain((RET_HEADS * RET_DV,))
    l0_ret_decay = ret_decay_base[None] + nrm((2, RET_HEADS), 0.05)
    l1_s5_lam_re = -0.5 + nrm((2, S5_GROUPS, S5_STATE), 0.01)
    l1_s5_lam_im = jnp.pi * s5_n + nrm((2, S5_GROUPS, S5_STATE), 0.01)
    l1_s5_log_dt = jax.random.uniform(next(ks), (2, S5_GROUPS), f32, math.log(1e-3), math.log(1e-1))
    l1_s5_b_re = nrm((2, S5_GROUPS, S5_STATE, S5_GROUP), (2 * S5_GROUP) ** -0.5)
    l1_s5_b_im = nrm((2, S5_GROUPS, S5_STATE, S5_GROUP), (2 * S5_GROUP) ** -0.5)
    l1_s5_c_re = nrm((2, S5_GROUPS, S5_GROUP, S5_STATE), (2 * S5_STATE) ** -0.5)
    l1_s5_c_im = nrm((2, S5_GROUPS, S5_GROUP, S5_STATE), (2 * S5_STATE) ** -0.5)
    l1_s5_d = nrm((D_MODEL,))
    l1_s5_w_glu = nrm((D_MODEL, 2 * D_MODEL), D_MODEL ** -0.5)
    l1_s5_b_glu = nrm((2 * D_MODEL,), 0.02)
    l2_da_w_qkv = nrm((D_MODEL, 3 * D_MODEL), D_MODEL ** -0.5)
    l2_da_w_o = nrm((D_MODEL, D_MODEL), D_MODEL ** -0.5)
    l2_da_subln = gain((D_MODEL,))
    l2_da_lambda = nrm((4, DA_DH), 0.1)
    l3_ret_w_in = nrm((D_MODEL, ret_in), D_MODEL ** -0.5)
    l3_ret_w_o = nrm((RET_HEADS * RET_DV, D_MODEL), (RET_HEADS * RET_DV) ** -0.5)
    l3_ret_gn = gain((RET_HEADS * RET_DV,))
    l3_ret_decay = ret_decay_base[None] + nrm((2, RET_HEADS), 0.05)
    return {"x_prompt": x_prompt, "x_sample": x_sample, "state_ret_l0": state_ret_l0,
            "state_s5_l1": state_s5_l1, "cache_k_l2": cache_k_l2, "cache_v_l2": cache_v_l2,
            "state_ret_l3": state_ret_l3, "c": c, "c_ctx": c_ctx, "ada_w": ada_w, "ada_b": ada_b,
            "norm_w": norm_w, "final_norm_w": final_norm_w, "router_w": router_w, "router_b": router_b,
            "moe_w13": moe_w13, "moe_w2": moe_w2,
            "l0_ret_w_in": l0_ret_w_in, "l0_ret_w_o": l0_ret_w_o, "l0_ret_gn": l0_ret_gn,
            "l0_ret_decay": l0_ret_decay,
            "l1_s5_lam_re": l1_s5_lam_re, "l1_s5_lam_im": l1_s5_lam_im, "l1_s5_log_dt": l1_s5_log_dt,
            "l1_s5_b_re": l1_s5_b_re, "l1_s5_b_im": l1_s5_b_im, "l1_s5_c_re": l1_s5_c_re,
            "l1_s5_c_im": l1_s5_c_im, "l1_s5_d": l1_s5_d, "l1_s5_w_glu": l1_s5_w_glu,
            "l1_s5_b_glu": l1_s5_b_glu,
            "l2_da_w_qkv": l2_da_w_qkv, "l2_da_w_o": l2_da_w_o, "l2_da_subln": l2_da_subln,
            "l2_da_lambda": l2_da_lambda,
            "l3_ret_w_in": l3_ret_w_in, "l3_ret_w_o": l3_ret_w_o, "l3_ret_gn": l3_ret_gn,
            "l3_ret_decay": l3_ret_decay}


def reference(x_prompt, x_sample, state_ret_l0, state_s5_l1, cache_k_l2, cache_v_l2, state_ret_l3,
              c, c_ctx, ada_w, ada_b, norm_w, final_norm_w, router_w, router_b, moe_w13, moe_w2,
              l0_ret_w_in, l0_ret_w_o, l0_ret_gn, l0_ret_decay,
              l1_s5_lam_re, l1_s5_lam_im, l1_s5_log_dt, l1_s5_b_re, l1_s5_b_im, l1_s5_c_re, l1_s5_c_im,
              l1_s5_d, l1_s5_w_glu, l1_s5_b_glu,
              l2_da_w_qkv, l2_da_w_o, l2_da_subln, l2_da_lambda,
              l3_ret_w_in, l3_ret_w_o, l3_ret_gn, l3_ret_decay):
    mixer_params = (
        (l0_ret_w_in, l0_ret_w_o, l0_ret_gn, l0_ret_decay),
        (l1_s5_lam_re, l1_s5_lam_im, l1_s5_log_dt, l1_s5_b_re, l1_s5_b_im, l1_s5_c_re, l1_s5_c_im,
         l1_s5_d, l1_s5_w_glu, l1_s5_b_glu),
        (l2_da_w_qkv, l2_da_w_o, l2_da_subln, l2_da_lambda),
        (l3_ret_w_in, l3_ret_w_o, l3_ret_gn, l3_ret_decay),
    )
    layer_cache = (state_ret_l0, state_s5_l1, (cache_k_l2, cache_v_l2), state_ret_l3)
    n_ctx_b = x_prompt.shape[0]
    rope_cos, rope_sin = axial_rope(x_sample.shape[1])
    x_c, x_l = x_prompt, x_sample
    ctx_state = []
    for i in range(DEPTH):
        kind = i % N_MIXERS
        params = mixer_params[i]
        cache = layer_cache[i]
        mc = modulation(c_ctx[None], ada_w[i], ada_b[i])
        ml = modulation(c, ada_w[i], ada_b[i])
        h_c = rms_norm(x_c, norm_w[i, 0]) * (1.0 + mc[1]) + mc[0]
        h_l = rms_norm(x_l, norm_w[i, 0]) * (1.0 + ml[1]) + ml[0]
        if kind == 0:
            zero = jnp.zeros((n_ctx_b, 2, RET_HEADS, RET_DK, RET_DV), jnp.float32)
            y_c, st = retention_mixer(h_c, *params, zero)
            y_l, _ = retention_mixer(h_l, *params, cache)
            ctx_state.append(st)
        elif kind == 1:
            zero = jnp.zeros((n_ctx_b, 2, S5_GROUPS, S5_STATE, 2), jnp.float32)
            y_c, st = s5_mixer(h_c, *params, zero)
            y_l, _ = s5_mixer(h_l, *params, cache)
            ctx_state.append(st)
        else:
            w_qkv, w_o, subln, lam_p = params
            lam_init = 0.8 - 0.6 * math.exp(-0.3 * i)
            lam = diff_lambda(lam_p, lam_init)
            q_c, k_c, v_c = diff_attn_project(h_c, w_qkv)
            y_c = diff_attn_out(diff_attention(q_c, k_c, v_c, lam), w_o, subln, lam_init)
            q_l, k_l, v_l = diff_attn_project(h_l, w_qkv)
            q_l = apply_rope(q_l, rope_cos, rope_sin)
            k_l = apply_rope(k_l, rope_cos, rope_sin)
            k_all = jnp.concatenate([cache[0].astype(k_l.dtype), k_l], axis=1)
            v_all = jnp.concatenate([cache[1].astype(v_l.dtype), v_l], axis=1)
            y_l = diff_attn_out(diff_attention(q_l, k_all, v_all, lam), w_o, subln, lam_init)
            ctx_state.append(k_c)
            ctx_state.append(v_c)
        x_c = x_c + mc[2] * y_c
        x_l = x_l + ml[2] * y_l
        h2_c = rms_norm(x_c, norm_w[i, 1]) * (1.0 + mc[4]) + mc[3]
        h2_l = rms_norm(x_l, norm_w[i, 1]) * (1.0 + ml[4]) + ml[3]
        x_c = x_c + mc[5] * moe_ffn(h2_c, router_w, router_b, moe_w13[i], moe_w2[i])
        x_l = x_l + ml[5] * moe_ffn(h2_l, router_w, router_b, moe_w13[i], moe_w2[i])
    y_prompt = rms_norm(x_c, final_norm_w)
    y_sample = rms_norm(x_l, final_norm_w)
    return (y_prompt, y_sample, ctx_state[0], ctx_state[1], ctx_state[2], ctx_state[3], ctx_state[4])
```

```python
import functools
import math

import jax
import jax.numpy as jnp
from jax import lax
from jax.experimental import pallas as pl
from jax.experimental.pallas import tpu as pltpu

F32 = jnp.float32
BF16 = jnp.bfloat16

D = 4096
CTX_B, CTX_L = 16, 256
LAT_B, LAT_L = 4, 1024
PAST = 512
N_CTX = CTX_B * CTX_L
N_LAT = LAT_B * LAT_L
N_TOK = N_CTX + N_LAT
DEPTH = 4
EPS = 1e-6

RET_H, RET_DK, RET_DV, RET_C = 16, 256, 512, 128
S5_GB = 32
S5_NS = 512
S5_TCH = 256
DA_H, DA_DH, DA_VD = 16, 128, 256
N_EXP, D_EXP = 16, 1024
GRID_W = 64
ROPE_BASE = 10000.0

MOE_TM = 512
MOE_TF = 128
MOE_NB = 2 * N_TOK // MOE_TM + N_EXP
MOE_ROWS = MOE_NB * MOE_TM
GATHER_RB = 256
COMBINE_TC = 128

MIB = 1024 * 1024
MOD_ROWS = 8
J_SHIFT1, J_SCALE1, J_GATE1, J_SHIFT2, J_SCALE2, J_GATE2 = range(6)


def _cp(n_axes, vmem_mib):
    return pltpu.CompilerParams(dimension_semantics=("arbitrary",) * n_axes,
                                vmem_limit_bytes=vmem_mib * MIB)


def _mod_row(i, tm):
    t0 = i * tm
    return jnp.where(t0 < N_CTX, 0, t0 // LAT_L - (N_CTX // LAT_L - 1))


def _mod_spec(tm, j, tn=D, col=None):
    if col is None:
        return pl.BlockSpec((None, None, 1, tn), lambda i, *_: (_mod_row(i, tm), j, 0, 0))
    return pl.BlockSpec((None, None, 1, tn), lambda i, *a: (_mod_row(i, tm), j, 0, col(i, *a)))


def _mod_kernel(c_ref, w_ref, b_ref, o_ref):
    c = c_ref[...]
    s = (c * jax.nn.sigmoid(c)).astype(BF16)
    o_ref[...] = jnp.dot(s, w_ref[...].astype(BF16), preferred_element_type=F32) + b_ref[...]


def modulation_all(cvecs, ada_w, ada_b):
    tn = 512
    n = 6 * D
    out = pl.pallas_call(
        _mod_kernel,
        out_shape=jax.ShapeDtypeStruct((DEPTH, MOD_ROWS, n), F32),
        grid=(DEPTH, n // tn),
        in_specs=[pl.BlockSpec((MOD_ROWS, D), lambda l, j: (0, 0)),
                  pl.BlockSpec((None, D, tn), lambda l, j: (l, 0, j)),
                  pl.BlockSpec((None, 1, tn), lambda l, j: (l, 0, j))],
        out_specs=pl.BlockSpec((None, MOD_ROWS, tn), lambda l, j: (l, 0, j)),
        compiler_params=_cp(2, 40),
        name="modulation",
    )(cvecs, ada_w, ada_b.reshape(DEPTH, 1, n))
    return out.reshape(DEPTH, MOD_ROWS, 6, 1, D)


def _normmod(x, g_ref, sh_ref, sc_ref):
    y = x * lax.rsqrt(jnp.mean(x * x, axis=-1, keepdims=True) + EPS)
    return y * g_ref[...] * (1.0 + sc_ref[...]) + sh_ref[...]


def _normmod_kernel(x_ref, g_ref, sh_ref, sc_ref, o_ref):
    o_ref[...] = _normmod(x_ref[...], g_ref, sh_ref, sc_ref).astype(o_ref.dtype)


def normmod(x, g, mod, j_shift, j_scale, out_dtype):
    tm = 256
    return pl.pallas_call(
        _normmod_kernel,
        out_shape=jax.ShapeDtypeStruct((N_TOK, D), out_dtype),
        grid=(N_TOK // tm,),
        in_specs=[pl.BlockSpec((tm, D), lambda i: (i, 0)),
                  pl.BlockSpec((1, D), lambda i: (0, 0)),
                  _mod_spec(tm, j_shift), _mod_spec(tm, j_scale)],
        out_specs=pl.BlockSpec((tm, D), lambda i: (i, 0)),
        compiler_params=_cp(1, 40),
        name="normmod",
    )(x, g.reshape(1, D), mod, mod)


def _row(a, r):
    return a[r:r + 1, :]


def _first_argmax(vals):
    best, idx = vals[0], jnp.zeros(vals[0].shape, jnp.int32)
    for j in range(1, len(vals)):
        better = vals[j] > best
        idx = jnp.where(better, j, idx)
        best = jnp.where(better, vals[j], best)
    return idx, best


def _select(vals, idx):
    out = vals[0]
    for j in range(1, len(vals)):
        out = jnp.where(idx == j, vals[j], out)
    return out


def _normmod_route_kernel(x_ref, g_ref, sh_ref, sc_ref, rwt_ref, rb_ref, h_ref, ids_ref, gates_ref):
    h = _normmod(x_ref[...], g_ref, sh_ref, sc_ref)
    h_ref[...] = h
    h_hi = h.astype(BF16)
    h_lo = (h - h_hi.astype(F32)).astype(BF16)
    w = rwt_ref[...]
    w_hi = w.astype(BF16)
    w_lo = (w - w_hi.astype(F32)).astype(BF16)
    dn = (((1,), (1,)), ((), ()))
    logits = (lax.dot_general(w_hi, h_hi, dn, preferred_element_type=F32)
              + lax.dot_general(w_hi, h_lo, dn, preferred_element_type=F32)
              + lax.dot_general(w_lo, h_hi, dn, preferred_element_type=F32))
    ex = jnp.exp(logits - jnp.max(logits, axis=0, keepdims=True))
    probs = ex / jnp.sum(ex, axis=0, keepdims=True)
    sel = probs + rb_ref[...]
    n_grp, epg = 4, 4
    scores = []
    for g in range(n_grp):
        a, b, c, d = (_row(sel, epg * g + j) for j in range(epg))
        m1, n1 = jnp.maximum(a, b), jnp.minimum(a, b)
        m2, n2 = jnp.maximum(c, d), jnp.minimum(c, d)
        scores.append(jnp.maximum(m1, m2) + jnp.maximum(jnp.minimum(m1, m2), jnp.maximum(n1, n2)))
    grp, _ = _first_argmax(scores)
    v = [_select([_row(sel, epg * g + j) for g in range(n_grp)], grp) for j in range(epg)]
    p = [_select([_row(probs, epg * g + j) for g in range(n_grp)], grp) for j in range(epg)]
    l1, _ = _first_argmax(v)
    neg = jnp.full(v[0].shape, -jnp.inf, F32)
    l2, _ = _first_argmax([jnp.where(l1 == j, neg, v[j]) for j in range(epg)])
    w1, w2 = _select(p, l1), _select(p, l2)
    tot = w1 + w2
    ids_ref[0:1, :] = grp * epg + l1
    ids_ref[1:2, :] = grp * epg + l2
    gates_ref[0:1, :] = w1 / tot
    gates_ref[1:2, :] = w2 / tot


def normmod_route(x, g, mod, router_wt, router_b):
    tm = 256
    return pl.pallas_call(
        _normmod_route_kernel,
        out_shape=(jax.ShapeDtypeStruct((N_TOK, D), F32),
                   jax.ShapeDtypeStruct((2, N_TOK), jnp.int32),
                   jax.ShapeDtypeStruct((2, N_TOK), F32)),
        grid=(N_TOK // tm,),
        in_specs=[pl.BlockSpec((tm, D), lambda i: (i, 0)),
                  pl.BlockSpec((1, D), lambda i: (0, 0)),
                  _mod_spec(tm, J_SHIFT2), _mod_spec(tm, J_SCALE2),
                  pl.BlockSpec((N_EXP, D), lambda i: (0, 0)),
                  pl.BlockSpec((N_EXP, 1), lambda i: (0, 0))],
        out_specs=(pl.BlockSpec((tm, D), lambda i: (i, 0)),
                   pl.BlockSpec((2, tm), lambda i: (0, i)),
                   pl.BlockSpec((2, tm), lambda i: (0, i))),
        compiler_params=_cp(1, 40),
        name="normmod_route",
    )(x, g.reshape(1, D), mod, mod, router_wt, router_b.reshape(N_EXP, 1))


def _mm_kernel(h_ref, w_ref, o_ref):
    o_ref[...] = jnp.dot(h_ref[...], w_ref[...].astype(BF16),
                         preferred_element_type=F32).astype(o_ref.dtype)


def matmul(h, w, out_dtype, tm=1024, tn=512):
    m, k = h.shape
    n = w.shape[1]
    return pl.pallas_call(
        _mm_kernel,
        out_shape=jax.ShapeDtypeStruct((m, n), out_dtype),
        grid=(m // tm, n // tn),
        in_specs=[pl.BlockSpec((tm, k), lambda i, j: (i, 0)),
                  pl.BlockSpec((k, tn), lambda i, j: (0, j))],
        out_specs=pl.BlockSpec((tm, tn), lambda i, j: (i, j)),
        compiler_params=_cp(2, 52),
        name="matmul",
    )(h, w)


def _mm_res_kernel(h_ref, w_ref, x_ref, gate_ref, o_ref, *, nk):
    y = jnp.dot(h_ref[...], w_ref[...].astype(BF16), preferred_element_type=F32)
    if nk == 1:
        o_ref[...] = x_ref[...] + gate_ref[...] * y
        return
    k = pl.program_id(2)

    @pl.when(k == 0)
    def _():
        o_ref[...] = y

    @pl.when(jnp.logical_and(k > 0, k < nk - 1))
    def _():
        o_ref[...] += y

    @pl.when(k == nk - 1)
    def _():
        o_ref[...] = x_ref[...] + gate_ref[...] * (o_ref[...] + y)


def matmul_residual(h, w, x, mod, j_gate, tm=1024, tn=512, tk=4096):
    m, k = h.shape
    n = w.shape[1]
    nk = k // tk
    return pl.pallas_call(
        functools.partial(_mm_res_kernel, nk=nk),
        out_shape=jax.ShapeDtypeStruct((m, n), F32),
        grid=(m // tm, n // tn, nk),
        in_specs=[pl.BlockSpec((tm, tk), lambda i, j, kk: (i, kk)),
                  pl.BlockSpec((tk, tn), lambda i, j, kk: (kk, j)),
                  pl.BlockSpec((tm, tn), lambda i, j, kk: (i, j)),
                  _mod_spec(tm, j_gate, tn, lambda i, j, kk: j)],
        out_specs=pl.BlockSpec((tm, tn), lambda i, j, kk: (i, j)),
        compiler_params=_cp(3, 52),
        name="matmul_residual",
    )(h, w, x, mod)


def _mm_glu_kernel(h_ref, wv_ref, wg_ref, bv_ref, bg_ref, x_ref, gate_ref, o_ref):
    h = h_ref[...]
    v = jnp.dot(h, wv_ref[...].astype(BF16), preferred_element_type=F32) + bv_ref[...]
    g = jnp.dot(h, wg_ref[...].astype(BF16), preferred_element_type=F32) + bg_ref[...]
    o_ref[...] = x_ref[...] + gate_ref[...] * (v * jax.nn.sigmoid(g))


def matmul_glu_residual(h, w, b, x, mod, j_gate, tm=1024, tn=256):
    m, k = h.shape
    n = w.shape[1] // 2
    nj = n // tn
    b2 = b.reshape(1, 2 * n)
    return pl.pallas_call(
        _mm_glu_kernel,
        out_shape=jax.ShapeDtypeStruct((m, n), F32),
        grid=(m // tm, nj),
        in_specs=[pl.BlockSpec((tm, k), lambda i, j: (i, 0)),
                  pl.BlockSpec((k, tn), lambda i, j: (0, j)),
                  pl.BlockSpec((k, tn), lambda i, j: (0, nj + j)),
                  pl.BlockSpec((1, tn), lambda i, j: (0, j)),
                  pl.BlockSpec((1, tn), lambda i, j: (0, nj + j)),
                  pl.BlockSpec((tm, tn), lambda i, j: (i, j)),
                  _mod_spec(tm, j_gate, tn, lambda i, j: j)],
        out_specs=pl.BlockSpec((tm, tn), lambda i, j: (i, j)),
        compiler_params=_cp(2, 52),
        name="matmul_glu",
    )(h, w, w, b2, b2, x, mod)


def _ret_kernel(lg_ref, q_ref, k_ref, v_ref, g_ref, gn_ref, *rest, n_chunks, has_s0, write_state):
    rest = list(rest)
    s0_ref = rest.pop(0) if has_s0 else None
    o_ref = rest.pop(0)
    st_ref = rest.pop(0) if write_state else None
    sf, sb, oacc = rest
    hd = pl.program_id(1)
    lgf, lgb = lg_ref[0, hd], lg_ref[1, hd]
    c = RET_C
    kscale = RET_DK ** -0.5
    ri = lax.broadcasted_iota(jnp.int32, (c, c), 0).astype(F32)
    ci = lax.broadcasted_iota(jnp.int32, (c, c), 1).astype(F32)
    rel = ri - ci
    dmask = (jnp.where(rel >= 0, jnp.exp(lgf * jnp.maximum(rel, 0.0)), 0.0)
             + jnp.where(rel <= 0, jnp.exp(lgb * jnp.maximum(-rel, 0.0)), 0.0)) * kscale
    pos = lax.broadcasted_iota(jnp.int32, (c, 1), 0).astype(F32)
    qd_f = jnp.exp(lgf * (pos + 1.0))
    kd_f = jnp.exp(lgf * (c - 1.0 - pos)) * kscale
    qd_b = jnp.exp(lgb * (c - pos))
    kd_b = jnp.exp(lgb * pos) * kscale
    one = jnp.ones((1, 1), F32)
    cd_f = jnp.exp(one * (lgf * c))
    cd_b = jnp.exp(one * (lgb * c))
    if has_s0:
        sf[...] = s0_ref[0]
        sb[...] = s0_ref[1]
    else:
        sf[...] = jnp.zeros(sf.shape, F32)
        sb[...] = jnp.zeros(sb.shape, F32)
    nt = (((1,), (1,)), ((), ()))

    def state_update(s_ref, q, k, v, qd, kd, cd):
        s = s_ref[...]
        o_cross = jnp.dot(q, s.astype(BF16), preferred_element_type=F32) * qd
        kdt = (k.astype(F32) * kd).T.astype(BF16)
        s_ref[...] = s * cd + jnp.dot(kdt, v, preferred_element_type=F32)
        return o_cross

    for ch in range(n_chunks):
        rows = pl.ds(ch * c, c)
        q, k, v = q_ref[rows, :], k_ref[rows, :], v_ref[rows, :]
        s = lax.dot_general(q, k, nt, preferred_element_type=F32)
        o = jnp.dot((s * dmask).astype(BF16), v, preferred_element_type=F32)
        oacc[rows, :] = o + state_update(sf, q, k, v, qd_f, kd_f, cd_f)
    for ch in reversed(range(n_chunks)):
        rows = pl.ds(ch * c, c)
        q, k, v = q_ref[rows, :], k_ref[rows, :], v_ref[rows, :]
        o = oacc[rows, :] + state_update(sb, q, k, v, qd_b, kd_b, cd_b)
        mu = jnp.mean(o, axis=-1, keepdims=True)
        dev = o - mu
        var = jnp.mean(dev * dev, axis=-1, keepdims=True)
        on = dev * lax.rsqrt(var + EPS) * gn_ref[...]
        gg = g_ref[rows, :].astype(F32)
        o_ref[rows, :] = ((gg * jax.nn.sigmoid(gg)) * on).astype(o_ref.dtype)
    if write_state:
        st_ref[0] = sf[...]
        st_ref[1] = sb[...]


def retention(qkvg, gn, log_gamma, s0, *, ctx):
    nb, L, blk0 = (CTX_B, CTX_L, 0) if ctx else (LAT_B, LAT_L, N_CTX // LAT_L)
    qcols, vcols = RET_H, 2 * RET_H * RET_DK // RET_DV
    in_specs = [pl.BlockSpec(memory_space=pltpu.SMEM),
                pl.BlockSpec((L, RET_DK), lambda b, h: (blk0 + b, h)),
                pl.BlockSpec((L, RET_DK), lambda b, h: (blk0 + b, qcols + h)),
                pl.BlockSpec((L, RET_DV), lambda b, h: (blk0 + b, vcols + h)),
                pl.BlockSpec((L, RET_DV), lambda b, h: (blk0 + b, vcols + RET_H + h)),
                pl.BlockSpec((1, RET_DV), lambda b, h: (0, h))]
    args = [log_gamma, qkvg, qkvg, qkvg, qkvg, gn.reshape(1, RET_H * RET_DV)]
    out_shape = [jax.ShapeDtypeStruct((nb * L, RET_H * RET_DV), BF16)]
    out_specs = [pl.BlockSpec((L, RET_DV), lambda b, h: (b, h))]
    if ctx:
        out_shape.append(jax.ShapeDtypeStruct((nb, 2, RET_H, RET_DK, RET_DV), F32))
        out_specs.append(pl.BlockSpec((None, 2, None, RET_DK, RET_DV), lambda b, h: (b, 0, h, 0, 0)))
    else:
        in_specs.append(pl.BlockSpec((None, 2, None, RET_DK, RET_DV), lambda b, h: (b, 0, h, 0, 0)))
        args.append(s0)
    return pl.pallas_call(
        functools.partial(_ret_kernel, n_chunks=L // RET_C, has_s0=not ctx, write_state=ctx),
        out_shape=tuple(out_shape),
        grid=(nb, RET_H),
        in_specs=in_specs,
        out_specs=tuple(out_specs),
        scratch_shapes=[pltpu.VMEM((RET_DK, RET_DV), F32), pltpu.VMEM((RET_DK, RET_DV), F32),
                        pltpu.VMEM((L, RET_DV), F32)],
        compiler_params=_cp(2, 40),
        name="retention_ctx" if ctx else "retention_lat",
    )(*args)


def _gelu_tanh(x):
    cdf = 0.5 * (1.0 + jnp.tanh(math.sqrt(2.0 / math.pi) * (x + 0.044715 * (x * x * x))))
    return x * cdf


def _s5_kernel(u_ref, wb_ref, wc_ref, a_ref, d_ref, *rest, L, has_h0, write_state):
    rest = list(rest)
    h0_ref = rest.pop(0) if has_h0 else None
    y_ref = rest.pop(0)
    st_ref = rest.pop(0) if write_state else None
    bu, yacc = rest
    tch, ns = S5_TCH, S5_NS
    nch = L // tch

    def u_rows(ch):
        return u_ref[pl.ds(ch * tch * 8, tch * 8), :]

    for d in (0, 1):
        a_re = jnp.broadcast_to(a_ref[d, :, 0:ns], (8, ns))
        a_im = jnp.broadcast_to(a_ref[d, :, ns:2 * ns], (8, ns))
        if has_h0:
            st = (h0_ref[d, :, 0:ns], h0_ref[d, :, ns:2 * ns])
        else:
            st = (jnp.zeros((8, ns), F32), jnp.zeros((8, ns), F32))
        for ch in (range(nch) if d == 0 else reversed(range(nch))):
            bu[...] = jnp.dot(u_rows(ch).astype(BF16), wb_ref[d], preferred_element_type=F32)

            def step(s, carry, d=d, a_re=a_re, a_im=a_im):
                t = s if d == 0 else tch - 1 - s
                r = pl.ds(pl.multiple_of(t * 8, 8), 8)
                sr, si = carry
                nr = a_re * sr - a_im * si + bu[r, 0:ns]
                ni = a_re * si + a_im * sr + bu[r, ns:2 * ns]
                bu[r, 0:ns] = nr
                bu[r, ns:2 * ns] = ni
                return nr, ni

            st = lax.fori_loop(0, tch, step, st, unroll=4)
            yc = jnp.dot(bu[...].astype(BF16), wc_ref[d], preferred_element_type=F32)
            rows = pl.ds(ch * tch * 8, tch * 8)
            if d == 0:
                yacc[rows, :] = yc
            else:
                yacc[rows, :] += yc
        if write_state:
            st_ref[d, :, 0:ns] = st[0]
            st_ref[d, :, ns:2 * ns] = st[1]
    for ch in range(nch):
        rows = pl.ds(ch * tch * 8, tch * 8)
        y = yacc[rows, :] + d_ref[...] * u_rows(ch)
        y_ref[rows, :] = _gelu_tanh(y).astype(y_ref.dtype)


def s5_scan(u, wb, wc, a_cat, d_skip, h0, *, L, nbt, write_state):
    has_h0 = h0 is not None
    st_spec = pl.BlockSpec((2, None, None, 8, 2 * S5_NS), lambda g, b: (0, g, b, 0, 0))
    in_specs = [pl.BlockSpec((None, L * 8, 128), lambda g, b: (b, 0, g)),
                pl.BlockSpec((2, None, 128, 2 * S5_NS), lambda g, b: (0, g, 0, 0)),
                pl.BlockSpec((2, None, 2 * S5_NS, 128), lambda g, b: (0, g, 0, 0)),
                pl.BlockSpec((2, None, 1, 2 * S5_NS), lambda g, b: (0, g, 0, 0)),
                pl.BlockSpec((1, 128), lambda g, b: (0, g))]
    args = [u, wb, wc, a_cat, d_skip.reshape(1, D)]
    if has_h0:
        in_specs.append(st_spec)
        args.append(h0)
    out_shape = [jax.ShapeDtypeStruct((nbt, L * 8, D), BF16)]
    out_specs = [pl.BlockSpec((None, L * 8, 128), lambda g, b: (b, 0, g))]
    if write_state:
        out_shape.append(jax.ShapeDtypeStruct((2, S5_GB, nbt, 8, 2 * S5_NS), F32))
        out_specs.append(st_spec)
    return pl.pallas_call(
        functools.partial(_s5_kernel, L=L, has_h0=has_h0, write_state=write_state),
        out_shape=tuple(out_shape),
        grid=(S5_GB, nbt),
        in_specs=in_specs,
        out_specs=tuple(out_specs),
        scratch_shapes=[pltpu.VMEM((S5_TCH * 8, 2 * S5_NS), F32), pltpu.VMEM((L * 8, 128), F32)],
        compiler_params=_cp(2, 48),
        name="s5_ctx" if write_state else "s5_lat",
    )(*args)


def s5_params(lam_re, lam_im, log_dt, b_re, b_im, c_re, c_im):
    dt = jnp.exp(log_dt)[..., None]
    dl_re, dl_im = lam_re * dt, lam_im * dt
    mag = jnp.exp(dl_re)
    a_re, a_im = mag * jnp.cos(dl_im), mag * jnp.sin(dl_im)
    den = lam_re * lam_re + lam_im * lam_im
    z_re = a_re - 1.0
    coef_re = (z_re * lam_re + a_im * lam_im) / den
    coef_im = (a_im * lam_re - z_re * lam_im) / den
    bb_re = coef_re[..., None] * b_re - coef_im[..., None] * b_im
    bb_im = coef_re[..., None] * b_im + coef_im[..., None] * b_re
    eye = jnp.eye(8, dtype=F32)

    def pack_b(bb):
        t = bb.reshape(2, S5_GB, 8, 64, 16).transpose(0, 1, 2, 4, 3)
        return (t[:, :, :, :, None, :] * eye[None, None, :, None, :, None]).reshape(2, S5_GB, 128, S5_NS)

    def pack_c(cc):
        t = cc.reshape(2, S5_GB, 8, 16, 64).transpose(0, 1, 2, 4, 3)
        return (t[:, :, :, :, None, :] * eye[None, None, :, None, :, None]).reshape(2, S5_GB, S5_NS, 128)

    wb = jnp.concatenate([pack_b(bb_re), pack_b(bb_im)], axis=-1).astype(BF16)
    wc = jnp.concatenate([pack_c(c_re), -pack_c(c_im)], axis=2).astype(BF16)
    a_cat = jnp.concatenate([a_re.reshape(2, S5_GB, S5_NS), a_im.reshape(2, S5_GB, S5_NS)], axis=-1)
    return wb, wc, a_cat.reshape(2, S5_GB, 1, 2 * S5_NS)


def _rope(x, cos, sin_signed):
    lane = lax.broadcasted_iota(jnp.int32, x.shape, 1)
    first_half = (lane % 64) < 32
    rot = jnp.where(first_half, pltpu.roll(x, 96, 1), pltpu.roll(x, 32, 1))
    return x * cos + rot * sin_signed


def _attn_kernel(lam_ref, q_ref, kn_ref, vn_ref, sub_ref, *rest, n_cache, rope, lam_init):
    rest = list(rest)
    if n_cache:
        ck_ref, cv_ref = rest.pop(0), rest.pop(0)
    if rope:
        cosq, sinq, cosk, sink = (rest.pop(0) for _ in range(4))
    o_ref, kall, vall = rest
    dh = DA_DH

    @pl.when(pl.program_id(2) == 0)
    def _():
        if n_cache:
            kall[0:n_cache, :] = ck_ref[...].astype(BF16)
            vall[0:n_cache, :] = cv_ref[...].astype(BF16)
        kn = kn_ref[...]
        if rope:
            kn = jnp.concatenate([_rope(kn[:, m * dh:(m + 1) * dh], cosk[...], sink[...]) for m in (0, 1)],
                                 axis=1)
        kall[n_cache:, :] = kn.astype(BF16)
        vall[n_cache:, :] = vn_ref[...].astype(BF16)

    q = q_ref[...]
    nt = (((1,), (1,)), ((), ()))
    probs = []
    for m in (0, 1):
        qm = q[:, m * dh:(m + 1) * dh]
        if rope:
            qm = _rope(qm, cosq[...], sinq[...])
        s = lax.dot_general(qm.astype(BF16), kall[:, m * dh:(m + 1) * dh], nt,
                            preferred_element_type=F32) * (dh ** -0.5)
        e = jnp.exp(s - jnp.max(s, axis=-1, keepdims=True))
        probs.append(e / jnp.sum(e, axis=-1, keepdims=True))
    p = (probs[0] - lam_ref[0] * probs[1]).astype(BF16)
    o = jnp.dot(p, vall[...], preferred_element_type=F32)
    of = o * lax.rsqrt(jnp.mean(o * o, axis=-1, keepdims=True) + EPS) * (1.0 - lam_init)
    o_ref[...] = (of * sub_ref[...]).astype(o_ref.dtype)


def diff_attention(qkv, subln, lam, lam_init, cache_k, cache_v, rope_tabs, *, ctx):
    tq = 256
    nb, L, blk0 = (CTX_B, CTX_L, 0) if ctx else (LAT_B, LAT_L, N_CTX // LAT_L)
    nq = L // tq
    qblk0 = blk0 * (LAT_L // tq) if not ctx else 0
    n_cache = 0 if ctx else PAST
    in_specs = [pl.BlockSpec(memory_space=pltpu.SMEM),
                pl.BlockSpec((tq, DA_VD), lambda b, h, qi: (qblk0 + b * nq + qi, h)),
                pl.BlockSpec((L, DA_VD), lambda b, h, qi: (blk0 + b, DA_H + h)),
                pl.BlockSpec((L, DA_VD), lambda b, h, qi: (blk0 + b, 2 * DA_H + h)),
                pl.BlockSpec((1, DA_VD), lambda b, h, qi: (0, h))]
    args = [lam, qkv, qkv, qkv, subln.reshape(1, D)]
    if not ctx:
        cache_spec = pl.BlockSpec((None, PAST, DA_VD), lambda b, h, qi: (b, 0, h))
        in_specs += [cache_spec, cache_spec]
        args += [cache_k.reshape(LAT_B, PAST, D), cache_v.reshape(LAT_B, PAST, D)]
        cos, sin_signed = rope_tabs
        in_specs += [pl.BlockSpec((tq, DA_DH), lambda b, h, qi: (qi, 0))] * 2
        in_specs += [pl.BlockSpec((L, DA_DH), lambda b, h, qi: (0, 0))] * 2
        args += [cos, sin_signed, cos, sin_signed]
    lk = n_cache + L
    return pl.pallas_call(
        functools.partial(_attn_kernel, n_cache=n_cache, rope=not ctx, lam_init=lam_init),
        out_shape=jax.ShapeDtypeStruct((nb * L, D), BF16),
        grid=(nb, DA_H, nq),
        in_specs=in_specs,
        out_specs=pl.BlockSpec((tq, DA_VD), lambda b, h, qi: (b * nq + qi, h)),
        scratch_shapes=[pltpu.VMEM((lk, 2 * DA_DH), BF16), pltpu.VMEM((lk, DA_VD), BF16)],
        compiler_params=_cp(3, 40),
        name="diff_attn_ctx" if ctx else "diff_attn_lat",
    )(*args)


def rope_tables(L):
    rows = L // GRID_W
    row = jnp.repeat(jnp.arange(rows, dtype=F32), GRID_W)
    col = jnp.tile(jnp.arange(GRID_W, dtype=F32), rows)
    n_freq = DA_DH // 4
    inv = ROPE_BASE ** (-jnp.arange(n_freq, dtype=F32) / n_freq)
    ang_r, ang_c = row[:, None] * inv, col[:, None] * inv
    cos = jnp.concatenate([jnp.cos(ang_r)] * 2 + [jnp.cos(ang_c)] * 2, axis=-1)
    sin_signed = jnp.concatenate([-jnp.sin(ang_r), jnp.sin(ang_r), -jnp.sin(ang_c), jnp.sin(ang_c)], axis=-1)
    return cos, sin_signed


def _gather_rows_kernel(tok_ref, h_hbm, o_ref, buf, sem, *, rb, nsteps):
    i = pl.program_id(0)

    def issue(step, slot):
        base = step * rb

        def body(r, carry):
            t = tok_ref[base + r]
            pltpu.make_async_copy(h_hbm.at[pl.ds(t, 1), :], buf.at[slot, pl.ds(r, 1), :], sem.at[slot]).start()
            return carry

        lax.fori_loop(0, rb, body, 0)

    @pl.when(i == 0)
    def _():
        issue(0, 0)

    @pl.when(i + 1 < nsteps)
    def _():
        issue(i + 1, (i + 1) % 2)

    slot = i % 2
    pltpu.make_async_copy(h_hbm.at[pl.ds(0, rb), :], buf.at[slot], sem.at[slot]).wait()
    o_ref[...] = buf[slot].astype(o_ref.dtype)


def gather_rows(h, row_tok):
    rb = GATHER_RB
    nsteps = MOE_ROWS // rb
    return pl.pallas_call(
        functools.partial(_gather_rows_kernel, rb=rb, nsteps=nsteps),
        out_shape=jax.ShapeDtypeStruct((MOE_ROWS, D), BF16),
        grid_spec=pltpu.PrefetchScalarGridSpec(
            num_scalar_prefetch=1,
            grid=(nsteps,),
            in_specs=[pl.BlockSpec(memory_space=pl.ANY)],
            out_specs=pl.BlockSpec((rb, D), lambda i, tok: (i, 0)),
            scratch_shapes=[pltpu.VMEM((2, rb, D), F32), pltpu.SemaphoreType.DMA((2,))]),
        compiler_params=_cp(1, 40),
        name="moe_gather",
    )(row_tok, h)


def _ffn_kernel(be_ref, nu_ref, x_ref, wg_ref, wu_ref, w2_ref, o_ref):
    i, j = pl.program_id(0), pl.program_id(1)
    used = i < nu_ref[0]

    @pl.when(used)
    def _():
        x = x_ref[...]
        g = jnp.dot(x, wg_ref[...].astype(BF16), preferred_element_type=F32)
        u = jnp.dot(x, wu_ref[...].astype(BF16), preferred_element_type=F32)
        a = ((g * jax.nn.sigmoid(g)) * u).astype(BF16)
        y = jnp.dot(a, w2_ref[...].astype(BF16), preferred_element_type=F32)

        @pl.when(j == 0)
        def _():
            o_ref[...] = y

        @pl.when(j > 0)
        def _():
            o_ref[...] += y

    @pl.when(jnp.logical_and(jnp.logical_not(used), j == 0))
    def _():
        o_ref[...] = jnp.zeros(o_ref.shape, F32)


def expert_ffn(xs, w13, w2, blk_e, n_used):
    tm, tf = MOE_TM, MOE_TF
    nj = D_EXP // tf

    def jj(i, j, nu):
        return jnp.where(i < nu[0], j, nj - 1)

    return pl.pallas_call(
        _ffn_kernel,
        out_shape=jax.ShapeDtypeStruct((MOE_ROWS, D), F32),
        grid_spec=pltpu.PrefetchScalarGridSpec(
            num_scalar_prefetch=2,
            grid=(MOE_NB, nj),
            in_specs=[pl.BlockSpec((tm, D), lambda i, j, be, nu: (i, 0)),
                      pl.BlockSpec((None, D, tf), lambda i, j, be, nu: (be[i], 0, jj(i, j, nu))),
                      pl.BlockSpec((None, D, tf), lambda i, j, be, nu: (be[i], 0, nj + jj(i, j, nu))),
                      pl.BlockSpec((None, tf, D), lambda i, j, be, nu: (be[i], jj(i, j, nu), 0))],
            out_specs=pl.BlockSpec((tm, D), lambda i, j, be, nu: (i, 0))),
        compiler_params=_cp(2, 52),
        name="moe_ffn",
    )(blk_e, n_used, xs, w13, w13, w2)


def _combine_kernel(pos_ref, x_ref, gt_ref, gate_ref, fw_ref, ys_hbm, o_ref, buf, sem, *, tc, nsteps, final):
    i = pl.program_id(0)

    def issue(step, slot):
        base = step * tc

        def body(r, carry):
            for kk in (0, 1):
                p = pos_ref[kk * N_TOK + base + r]
                pltpu.make_async_copy(ys_hbm.at[pl.ds(p, 1), :], buf.at[slot, kk, pl.ds(r, 1), :],
                                      sem.at[slot]).start()
            return carry

        lax.fori_loop(0, tc, body, 0)

    @pl.when(i == 0)
    def _():
        issue(0, 0)

    @pl.when(i + 1 < nsteps)
    def _():
        issue(i + 1, (i + 1) % 2)

    slot = i % 2
    for kk in (0, 1):
        pltpu.make_async_copy(ys_hbm.at[pl.ds(0, tc), :], buf.at[slot, kk], sem.at[slot]).wait()
    gt = gt_ref[...]
    moe = gt[:, 0:1] * buf[slot, 0] + gt[:, 1:2] * buf[slot, 1]
    x2 = x_ref[...] + gate_ref[...] * moe
    if final:
        x2 = x2 * lax.rsqrt(jnp.mean(x2 * x2, axis=-1, keepdims=True) + EPS) * fw_ref[...]
    o_ref[...] = x2


def moe_combine(x, ys, pos, gates_t, mod, final_w, *, final):
    tc = COMBINE_TC
    nsteps = N_TOK // tc
    return pl.pallas_call(
        functools.partial(_combine_kernel, tc=tc, nsteps=nsteps, final=final),
        out_shape=jax.ShapeDtypeStruct((N_TOK, D), F32),
        grid_spec=pltpu.PrefetchScalarGridSpec(
            num_scalar_prefetch=1,
            grid=(nsteps,),
            in_specs=[pl.BlockSpec((tc, D), lambda i, pos: (i, 0)),
                      pl.BlockSpec((tc, 2), lambda i, pos: (i, 0)),
                      _mod_spec(tc, J_GATE2),
                      pl.BlockSpec((1, D), lambda i, pos: (0, 0)),
                      pl.BlockSpec(memory_space=pl.ANY)],
            out_specs=pl.BlockSpec((tc, D), lambda i, pos: (i, 0)),
            scratch_shapes=[pltpu.VMEM((2, 2, tc, D), F32), pltpu.SemaphoreType.DMA((2,))]),
        compiler_params=_cp(1, 40),
        name="moe_combine",
    )(pos, x, gates_t, mod, final_w.reshape(1, D), ys)


def dispatch_plan(ids):
    e_flat = ids.reshape(-1)
    onehot = (e_flat[:, None] == jnp.arange(N_EXP, dtype=jnp.int32)[None, :]).astype(jnp.int32)
    csum = jnp.cumsum(onehot, axis=0)
    rank = jnp.sum((csum - onehot) * onehot, axis=1)
    counts = csum[-1]
    nblk = (counts + MOE_TM - 1) // MOE_TM
    blk_end = jnp.cumsum(nblk)
    n_used = blk_end[-1]
    pos = ((blk_end - nblk)[e_flat] * MOE_TM + rank).astype(jnp.int32)
    tok = jnp.tile(jnp.arange(N_TOK, dtype=jnp.int32), 2)
    row_tok = jnp.zeros((MOE_ROWS,), jnp.int32).at[pos].set(tok)
    blk = jnp.minimum(jnp.arange(MOE_NB, dtype=jnp.int32), n_used - 1)
    blk_e = jnp.minimum(jnp.searchsorted(blk_end, blk, side="right"), N_EXP - 1).astype(jnp.int32)
    return pos, row_tok, blk_e, n_used.reshape(1).astype(jnp.int32)


def moe_layer(x, norm_g, mod, router_wt, router_b, w13, w2, final_w, *, final):
    h2, ids, gates = normmod_route(x, norm_g, mod, router_wt, router_b)
    pos, row_tok, blk_e, n_used = dispatch_plan(ids)
    xs = gather_rows(h2, row_tok)
    ys = expert_ffn(xs, w13, w2, blk_e, n_used)
    return moe_combine(x, ys, pos, gates.T, mod, final_w, final=final)


def _to_time_major(h):
    hc = h[:N_CTX].reshape(CTX_B // 8, 8, CTX_L, D).transpose(0, 2, 1, 3).reshape(CTX_B // 8, CTX_L * 8, D)
    hl = h[N_CTX:].reshape(LAT_B, LAT_L, D).transpose(1, 0, 2)
    hl = jnp.pad(hl, ((0, 0), (0, 8 - LAT_B), (0, 0))).reshape(1, LAT_L * 8, D)
    return hc, hl


def _from_time_major(yc, yl):
    yc = yc.reshape(CTX_B // 8, CTX_L, 8, D).transpose(0, 2, 1, 3).reshape(N_CTX, D)
    yl = yl.reshape(LAT_L, 8, D)[:, :LAT_B].transpose(1, 0, 2).reshape(N_LAT, D)
    return jnp.concatenate([yc, yl], axis=0)


def kernel(x_prompt, x_sample, state_ret_l0, state_s5_l1, cache_k_l2, cache_v_l2, state_ret_l3, c, c_ctx, ada_w, ada_b, norm_w, final_norm_w, router_w, router_b, moe_w13, moe_w2, l0_ret_w_in, l0_ret_w_o, l0_ret_gn, l0_ret_decay, l1_s5_lam_re, l1_s5_lam_im, l1_s5_log_dt, l1_s5_b_re, l1_s5_b_im, l1_s5_c_re, l1_s5_c_im, l1_s5_d, l1_s5_w_glu, l1_s5_b_glu, l2_da_w_qkv, l2_da_w_o, l2_da_subln, l2_da_lambda, l3_ret_w_in, l3_ret_w_o, l3_ret_gn, l3_ret_decay):
    x = jnp.concatenate([x_prompt.reshape(N_CTX, D), x_sample.reshape(N_LAT, D)], axis=0)
    cvecs = jnp.concatenate([c_ctx[None], c, jnp.zeros((MOD_ROWS - 1 - LAT_B, D), F32)], axis=0)
    mods = modulation_all(cvecs, ada_w, ada_b)
    router_wt = router_w.T

    def retention_layer(i, x, w_in, w_o, gn, decay, s0):
        mod = mods[i]
        h = normmod(x, norm_w[i, 0], mod, J_SHIFT1, J_SCALE1, BF16)
        qkvg = matmul(h, w_in, BF16)
        log_gamma = -jnp.exp(decay)
        o_c, st = retention(qkvg, gn, log_gamma, None, ctx=True)
        (o_l,) = retention(qkvg, gn, log_gamma, s0, ctx=False)
        o = jnp.concatenate([o_c, o_l], axis=0)
        return matmul_residual(o, w_o, x, mod, J_GATE1), st

    def moe(i, x):
        return moe_layer(x, norm_w[i, 1], mods[i], router_wt, router_b, moe_w13[i], moe_w2[i], final_norm_w,
                         final=(i == DEPTH - 1))

    x, st_ret0 = retention_layer(0, x, l0_ret_w_in, l0_ret_w_o, l0_ret_gn, l0_ret_decay, state_ret_l0)
    x = moe(0, x)

    mod = mods[1]
    h = normmod(x, norm_w[1, 0], mod, J_SHIFT1, J_SCALE1, F32)
    wb, wc, a_cat = s5_params(l1_s5_lam_re, l1_s5_lam_im, l1_s5_log_dt, l1_s5_b_re, l1_s5_b_im,
                              l1_s5_c_re, l1_s5_c_im)
    u_c, u_l = _to_time_major(h)
    h0 = state_s5_l1.reshape(LAT_B, 2, S5_GB, 8, 64, 2).transpose(1, 2, 0, 5, 3, 4)
    h0 = jnp.pad(h0.reshape(2, S5_GB, LAT_B, 2 * S5_NS), ((0, 0), (0, 0), (0, 8 - LAT_B), (0, 0)))
    y_c, st5 = s5_scan(u_c, wb, wc, a_cat, l1_s5_d, None, L=CTX_L, nbt=CTX_B // 8, write_state=True)
    (y_l,) = s5_scan(u_l, wb, wc, a_cat, l1_s5_d, h0.reshape(2, S5_GB, 1, 8, 2 * S5_NS),
                     L=LAT_L, nbt=1, write_state=False)
    y = _from_time_major(y_c, y_l)
    x = matmul_glu_residual(y, l1_s5_w_glu, l1_s5_b_glu, x, mod, J_GATE1)
    st_s5 = st5.reshape(2, S5_GB, CTX_B // 8, 8, 2, 8, 64).transpose(2, 3, 0, 1, 5, 6, 4)
    st_s5 = st_s5.reshape(CTX_B, 2, S5_GB * 8, 64, 2)
    x = moe(1, x)

    mod = mods[2]
    lam_init = 0.8 - 0.6 * math.exp(-0.3 * 2)
    lp = l2_da_lambda
    lam = (jnp.exp(jnp.sum(lp[0] * lp[1])) - jnp.exp(jnp.sum(lp[2] * lp[3])) + lam_init).reshape(1)
    h = normmod(x, norm_w[2, 0], mod, J_SHIFT1, J_SCALE1, BF16)
    qkv = matmul(h, l2_da_w_qkv, F32)
    a_c = diff_attention(qkv, l2_da_subln, lam, lam_init, None, None, None, ctx=True)
    a_l = diff_attention(qkv, l2_da_subln, lam, lam_init, cache_k_l2, cache_v_l2, rope_tables(LAT_L), ctx=False)
    x = matmul_residual(jnp.concatenate([a_c, a_l], axis=0), l2_da_w_o, x, mod, J_GATE1)
    new_k = qkv[:N_CTX, D:2 * D].reshape(CTX_B, CTX_L, DA_H, 2, DA_DH)
    new_v = qkv[:N_CTX, 2 * D:].reshape(CTX_B, CTX_L, DA_H, DA_VD)
    x = moe(2, x)

    x, st_ret3 = retention_layer(3, x, l3_ret_w_in, l3_ret_w_o, l3_ret_gn, l3_ret_decay, state_ret_l3)
    y = moe(3, x)

    y_prompt = y[:N_CTX].reshape(CTX_B, CTX_L, D)
    y_sample = y[N_CTX:].reshape(LAT_B, LAT_L, D)
    return (y_prompt, y_sample, st_ret0, st_s5, new_k, new_v, st_ret3)
```

```python
import functools
import math

import jax
import jax.numpy as jnp
from jax import lax
from jax.experimental import pallas as pl
from jax.experimental.pallas import tpu as pltpu

F32 = jnp.float32
BF16 = jnp.bfloat16

D = 4096
CTX_B, CTX_L = 16, 256
LAT_B, LAT_L = 4, 1024
PAST = 512
N_CTX = CTX_B * CTX_L
N_LAT = LAT_B * LAT_L
N_TOK = N_CTX + N_LAT
DEPTH = 4
EPS = 1e-6

RET_H, RET_DK, RET_DV = 16, 256, 512
RET_C = 128
S5_GB = 32
S5_NS = 512
S5_CHUNK_ROWS = 2048
DA_H, DA_DH, DA_VD = 16, 128, 256
N_EXP, D_EXP = 16, 1024
GRID_W = 64
ROPE_BASE = 10000.0

MOE_TM = 512
MOE_TF = 256
MOE_TN = 2048
MOE_NB = 2 * N_TOK // MOE_TM + N_EXP
MOE_ROWS = MOE_NB * MOE_TM
GATHER_RB = 256
COMBINE_TC = 128

MIB = 1024 * 1024
MOD_ROWS = 8
J_SHIFT1, J_SCALE1, J_GATE1, J_SHIFT2, J_SCALE2, J_GATE2 = range(6)


def _cp(n_axes, vmem_mib):
    return pltpu.CompilerParams(dimension_semantics=("arbitrary",) * n_axes,
                                vmem_limit_bytes=vmem_mib * MIB)


def _mod_row(i, tm):
    t0 = i * tm
    return jnp.where(t0 < N_CTX, 0, t0 // LAT_L - (N_CTX // LAT_L - 1))


def _mod_spec(tm, j, tn=D, col=None):
    if col is None:
        return pl.BlockSpec((None, None, 1, tn), lambda i, *_: (_mod_row(i, tm), j, 0, 0))
    return pl.BlockSpec((None, None, 1, tn), lambda i, *a: (_mod_row(i, tm), j, 0, col(i, *a)))


def _mod_kernel(c_ref, w_ref, b_ref, o_ref):
    c = c_ref[...]
    s = (c * jax.nn.sigmoid(c)).astype(BF16)
    o_ref[...] = jnp.dot(s, w_ref[...].astype(BF16), preferred_element_type=F32) + b_ref[...]


def modulation_all(cvecs, ada_w, ada_b):
    tn = 512
    n = 6 * D
    out = pl.pallas_call(
        _mod_kernel,
        out_shape=jax.ShapeDtypeStruct((DEPTH, MOD_ROWS, n), F32),
        grid=(DEPTH, n // tn),
        in_specs=[pl.BlockSpec((MOD_ROWS, D), lambda l, j: (0, 0)),
                  pl.BlockSpec((None, D, tn), lambda l, j: (l, 0, j)),
                  pl.BlockSpec((None, 1, tn), lambda l, j: (l, 0, j))],
        out_specs=pl.BlockSpec((None, MOD_ROWS, tn), lambda l, j: (l, 0, j)),
        compiler_params=_cp(2, 40),
        name="modulation",
    )(cvecs, ada_w, ada_b.reshape(DEPTH, 1, n))
    return out.reshape(DEPTH, MOD_ROWS, 6, 1, D)


def _normmod(x, g_ref, sh_ref, sc_ref):
    y = x * lax.rsqrt(jnp.mean(x * x, axis=-1, keepdims=True) + EPS)
    return y * g_ref[...] * (1.0 + sc_ref[...]) + sh_ref[...]


def _normmod_kernel(x_ref, g_ref, sh_ref, sc_ref, o_ref):
    o_ref[...] = _normmod(x_ref[...], g_ref, sh_ref, sc_ref).astype(o_ref.dtype)


def normmod(x, g, mod, j_shift, j_scale, out_dtype):
    tm = 256
    return pl.pallas_call(
        _normmod_kernel,
        out_shape=jax.ShapeDtypeStruct((N_TOK, D), out_dtype),
        grid=(N_TOK // tm,),
        in_specs=[pl.BlockSpec((tm, D), lambda i: (i, 0)),
                  pl.BlockSpec((1, D), lambda i: (0, 0)),
                  _mod_spec(tm, j_shift), _mod_spec(tm, j_scale)],
        out_specs=pl.BlockSpec((tm, D), lambda i: (i, 0)),
        compiler_params=_cp(1, 40),
        name="normmod",
    )(x, g.reshape(1, D), mod, mod)


def _row(a, r):
    return a[r:r + 1, :]


def _first_argmax(vals):
    best, idx = vals[0], jnp.zeros(vals[0].shape, jnp.int32)
    for j in range(1, len(vals)):
        better = vals[j] > best
        idx = jnp.where(better, j, idx)
        best = jnp.where(better, vals[j], best)
    return idx, best


def _select(vals, idx):
    out = vals[0]
    for j in range(1, len(vals)):
        out = jnp.where(idx == j, vals[j], out)
    return out


def _normmod_route_kernel(x_ref, g_ref, sh_ref, sc_ref, rwt_ref, rb_ref, h_ref, ids_ref, gates_ref,
                          rank_ref, cnt_ref):
    h = _normmod(x_ref[...], g_ref, sh_ref, sc_ref)
    h_ref[...] = h
    logits = lax.dot_general(rwt_ref[...].astype(BF16), h.astype(BF16), (((1,), (1,)), ((), ())),
                             preferred_element_type=F32)
    ex = jnp.exp(logits - jnp.max(logits, axis=0, keepdims=True))
    probs = ex / jnp.sum(ex, axis=0, keepdims=True)
    sel = probs + rb_ref[...]
    n_grp, epg = 4, 4
    scores = []
    for g in range(n_grp):
        a, b, c, d = (_row(sel, epg * g + j) for j in range(epg))
        m1, n1 = jnp.maximum(a, b), jnp.minimum(a, b)
        m2, n2 = jnp.maximum(c, d), jnp.minimum(c, d)
        scores.append(jnp.maximum(m1, m2) + jnp.maximum(jnp.minimum(m1, m2), jnp.maximum(n1, n2)))
    grp, _ = _first_argmax(scores)
    v = [_select([_row(sel, epg * g + j) for g in range(n_grp)], grp) for j in range(epg)]
    p = [_select([_row(probs, epg * g + j) for g in range(n_grp)], grp) for j in range(epg)]
    l1, _ = _first_argmax(v)
    neg = jnp.full(v[0].shape, -jnp.inf, F32)
    l2, _ = _first_argmax([jnp.where(l1 == j, neg, v[j]) for j in range(epg)])
    w1, w2 = _select(p, l1), _select(p, l2)
    tot = w1 + w2
    id1, id2 = grp * epg + l1, grp * epg + l2
    ids_ref[0:1, :] = id1
    ids_ref[1:2, :] = id2
    gates_ref[0:1, :] = w1 / tot
    gates_ref[1:2, :] = w2 / tot
    tm = id1.shape[1]
    eid = lax.broadcasted_iota(jnp.int32, (N_EXP, tm), 0)
    pick1, pick2 = eid == id1, eid == id2
    onehot = jnp.where(pick1, 1.0, 0.0) + jnp.where(pick2, 1.0, 0.0)
    before = lax.broadcasted_iota(jnp.int32, (tm, tm), 0) < lax.broadcasted_iota(jnp.int32, (tm, tm), 1)
    tri = jnp.where(before, 1.0, 0.0).astype(BF16)
    prefix = jnp.dot(onehot.astype(BF16), tri, preferred_element_type=F32)
    rank_ref[0:1, :] = jnp.sum(jnp.where(pick1, prefix, 0.0), axis=0, keepdims=True).astype(jnp.int32)
    rank_ref[1:2, :] = jnp.sum(jnp.where(pick2, prefix, 0.0), axis=0, keepdims=True).astype(jnp.int32)
    cnt_ref[...] = jnp.sum(onehot, axis=1, keepdims=True).astype(jnp.int32)


ROUTE_TM = 256


def normmod_route(x, g, mod, router_wt, router_b):
    tm = ROUTE_TM
    nt = N_TOK // tm
    return pl.pallas_call(
        _normmod_route_kernel,
        out_shape=(jax.ShapeDtypeStruct((N_TOK, D), F32),
                   jax.ShapeDtypeStruct((2, N_TOK), jnp.int32),
                   jax.ShapeDtypeStruct((2, N_TOK), F32),
                   jax.ShapeDtypeStruct((2, N_TOK), jnp.int32),
                   jax.ShapeDtypeStruct((nt, N_EXP, 1), jnp.int32)),
        grid=(nt,),
        in_specs=[pl.BlockSpec((tm, D), lambda i: (i, 0)),
                  pl.BlockSpec((1, D), lambda i: (0, 0)),
                  _mod_spec(tm, J_SHIFT2), _mod_spec(tm, J_SCALE2),
                  pl.BlockSpec((N_EXP, D), lambda i: (0, 0)),
                  pl.BlockSpec((N_EXP, 1), lambda i: (0, 0))],
        out_specs=(pl.BlockSpec((tm, D), lambda i: (i, 0)),
                   pl.BlockSpec((2, tm), lambda i: (0, i)),
                   pl.BlockSpec((2, tm), lambda i: (0, i)),
                   pl.BlockSpec((2, tm), lambda i: (0, i)),
                   pl.BlockSpec((None, N_EXP, 1), lambda i: (i, 0, 0))),
        compiler_params=_cp(1, 40),
        name="normmod_route",
    )(x, g.reshape(1, D), mod, mod, router_wt, router_b.reshape(N_EXP, 1))


def _mm_kernel(h_ref, w_ref, o_ref):
    o_ref[...] = jnp.dot(h_ref[...], w_ref[...].astype(BF16),
                         preferred_element_type=F32).astype(o_ref.dtype)


def matmul(h, w, out_dtype, tm=1024, tn=512):
    m, k = h.shape
    n = w.shape[1]
    return pl.pallas_call(
        _mm_kernel,
        out_shape=jax.ShapeDtypeStruct((m, n), out_dtype),
        grid=(m // tm, n // tn),
        in_specs=[pl.BlockSpec((tm, k), lambda i, j: (i, 0)),
                  pl.BlockSpec((k, tn), lambda i, j: (0, j))],
        out_specs=pl.BlockSpec((tm, tn), lambda i, j: (i, j)),
        compiler_params=_cp(2, 52),
        name="matmul",
    )(h, w)


def _mm_res_kernel(hc_ref, hl_ref, w_ref, x_ref, gate_ref, o_ref, *, n_ctx_tiles):
    def run(h_ref):
        y = jnp.dot(h_ref[...], w_ref[...].astype(BF16), preferred_element_type=F32)
        o_ref[...] = x_ref[...] + gate_ref[...] * y

    i = pl.program_id(0)

    @pl.when(i < n_ctx_tiles)
    def _():
        run(hc_ref)

    @pl.when(i >= n_ctx_tiles)
    def _():
        run(hl_ref)


def matmul_residual(h_ctx, h_lat, w, x, mod, j_gate, tm=1024, tn=512):
    k = h_ctx.shape[1]
    n = w.shape[1]
    nc = N_CTX // tm
    return pl.pallas_call(
        functools.partial(_mm_res_kernel, n_ctx_tiles=nc),
        out_shape=jax.ShapeDtypeStruct((N_TOK, n), F32),
        grid=(N_TOK // tm, n // tn),
        in_specs=[pl.BlockSpec((tm, k), lambda i, j: (jnp.minimum(i, nc - 1), 0)),
                  pl.BlockSpec((tm, k), lambda i, j: (jnp.maximum(i - nc, 0), 0)),
                  pl.BlockSpec((k, tn), lambda i, j: (0, j)),
                  pl.BlockSpec((tm, tn), lambda i, j: (i, j)),
                  _mod_spec(tm, j_gate, tn, lambda i, j: j)],
        out_specs=pl.BlockSpec((tm, tn), lambda i, j: (i, j)),
        compiler_params=_cp(2, 56),
        name="matmul_residual",
    )(h_ctx, h_lat, w, x, mod)


def _mm_glu_kernel(h_ref, wv_ref, wg_ref, bv_ref, bg_ref, x_ref, gate_ref, o_ref):
    h = h_ref[...]
    v = jnp.dot(h, wv_ref[...].astype(BF16), preferred_element_type=F32) + bv_ref[...]
    g = jnp.dot(h, wg_ref[...].astype(BF16), preferred_element_type=F32) + bg_ref[...]
    o_ref[...] = x_ref[...] + gate_ref[...] * (v * jax.nn.sigmoid(g))


def matmul_glu_residual(h, w, b, x, mod, j_gate, tm=1024, tn=256):
    m, k = h.shape
    n = w.shape[1] // 2
    nj = n // tn
    b2 = b.reshape(1, 2 * n)
    return pl.pallas_call(
        _mm_glu_kernel,
        out_shape=jax.ShapeDtypeStruct((m, n), F32),
        grid=(m // tm, nj),
        in_specs=[pl.BlockSpec((tm, k), lambda i, j: (i, 0)),
                  pl.BlockSpec((k, tn), lambda i, j: (0, j)),
                  pl.BlockSpec((k, tn), lambda i, j: (0, nj + j)),
                  pl.BlockSpec((1, tn), lambda i, j: (0, j)),
                  pl.BlockSpec((1, tn), lambda i, j: (0, nj + j)),
                  pl.BlockSpec((tm, tn), lambda i, j: (i, j)),
                  _mod_spec(tm, j_gate, tn, lambda i, j: j)],
        out_specs=pl.BlockSpec((tm, tn), lambda i, j: (i, j)),
        compiler_params=_cp(2, 52),
        name="matmul_glu",
    )(h, w, w, b2, b2, x, mod)


def _ret_kernel(lg_ref, q_ref, k_ref, v_ref, g_ref, gn_ref, *rest, n_chunks, has_s0, write_state):
    rest = list(rest)
    s0_ref = rest.pop(0) if has_s0 else None
    o_ref = rest.pop(0)
    st_ref = rest.pop(0) if write_state else None
    sf, sb, oacc = rest
    hd = pl.program_id(1)
    lgf, lgb = lg_ref[0, hd], lg_ref[1, hd]
    c = RET_C
    kscale = RET_DK ** -0.5
    ri = lax.broadcasted_iota(jnp.int32, (c, c), 0).astype(F32)
    ci = lax.broadcasted_iota(jnp.int32, (c, c), 1).astype(F32)
    rel = ri - ci
    dmask = (jnp.where(rel >= 0, jnp.exp(lgf * jnp.maximum(rel, 0.0)), 0.0)
             + jnp.where(rel <= 0, jnp.exp(lgb * jnp.maximum(-rel, 0.0)), 0.0)) * kscale
    pos = lax.broadcasted_iota(jnp.int32, (c, 1), 0).astype(F32)
    qd_f = jnp.exp(lgf * (pos + 1.0))
    kd_f = jnp.exp(lgf * (c - 1.0 - pos)) * kscale
    qd_b = jnp.exp(lgb * (c - pos))
    kd_b = jnp.exp(lgb * pos) * kscale
    one = jnp.ones((1, 1), F32)
    cd_f = jnp.exp(one * (lgf * c))
    cd_b = jnp.exp(one * (lgb * c))
    if has_s0:
        sf[...] = s0_ref[0]
        sb[...] = s0_ref[1]
    else:
        sf[...] = jnp.zeros(sf.shape, F32)
        sb[...] = jnp.zeros(sb.shape, F32)
    nt = (((1,), (1,)), ((), ()))

    def state_update(s_ref, q, k, v, qd, kd, cd):
        s = s_ref[...]
        o_cross = jnp.dot(q, s.astype(BF16), preferred_element_type=F32) * qd
        kdt = (k.astype(F32) * kd).T.astype(BF16)
        s_ref[...] = s * cd + jnp.dot(kdt, v, preferred_element_type=F32)
        return o_cross

    for ch in range(n_chunks):
        rows = pl.ds(ch * c, c)
        q, k, v = q_ref[rows, :].astype(BF16), k_ref[rows, :], v_ref[rows, :].astype(BF16)
        s = lax.dot_general(q, k.astype(BF16), nt, preferred_element_type=F32)
        o = jnp.dot((s * dmask).astype(BF16), v, preferred_element_type=F32)
        oacc[rows, :] = o + state_update(sf, q, k, v, qd_f, kd_f, cd_f)
    for ch in reversed(range(n_chunks)):
        rows = pl.ds(ch * c, c)
        q, k, v = q_ref[rows, :].astype(BF16), k_ref[rows, :], v_ref[rows, :].astype(BF16)
        o = oacc[rows, :] + state_update(sb, q, k, v, qd_b, kd_b, cd_b)
        mu = jnp.mean(o, axis=-1, keepdims=True)
        dev = o - mu
        var = jnp.mean(dev * dev, axis=-1, keepdims=True)
        on = dev * lax.rsqrt(var + EPS) * gn_ref[...]
        gg = g_ref[rows, :].astype(F32)
        o_ref[rows, :] = ((gg * jax.nn.sigmoid(gg)) * on).astype(o_ref.dtype)
    if write_state:
        st_ref[0] = sf[...]
        st_ref[1] = sb[...]


def retention(qkvg, gn, log_gamma, s0, *, ctx):
    nb, L, blk0 = (CTX_B, CTX_L, 0) if ctx else (LAT_B, LAT_L, N_CTX // LAT_L)
    qcols, vcols = RET_H, 2 * RET_H * RET_DK // RET_DV
    in_specs = [pl.BlockSpec(memory_space=pltpu.SMEM),
                pl.BlockSpec((L, RET_DK), lambda b, h: (blk0 + b, h)),
                pl.BlockSpec((L, RET_DK), lambda b, h: (blk0 + b, qcols + h)),
                pl.BlockSpec((L, RET_DV), lambda b, h: (blk0 + b, vcols + h)),
                pl.BlockSpec((L, RET_DV), lambda b, h: (blk0 + b, vcols + RET_H + h)),
                pl.BlockSpec((1, RET_DV), lambda b, h: (0, h))]
    args = [log_gamma, qkvg, qkvg, qkvg, qkvg, gn.reshape(1, RET_H * RET_DV)]
    out_shape = [jax.ShapeDtypeStruct((nb * L, RET_H * RET_DV), BF16)]
    out_specs = [pl.BlockSpec((L, RET_DV), lambda b, h: (b, h))]
    if ctx:
        out_shape.append(jax.ShapeDtypeStruct((nb, 2, RET_H, RET_DK, RET_DV), F32))
        out_specs.append(pl.BlockSpec((None, 2, None, RET_DK, RET_DV), lambda b, h: (b, 0, h, 0, 0)))
    else:
        in_specs.append(pl.BlockSpec((None, 2, None, RET_DK, RET_DV), lambda b, h: (b, 0, h, 0, 0)))
        args.append(s0)
    return pl.pallas_call(
        functools.partial(_ret_kernel, n_chunks=L // RET_C, has_s0=not ctx, write_state=ctx),
        out_shape=tuple(out_shape),
        grid=(nb, RET_H),
        in_specs=in_specs,
        out_specs=tuple(out_specs),
        scratch_shapes=[pltpu.VMEM((RET_DK, RET_DV), F32), pltpu.VMEM((RET_DK, RET_DV), F32),
                        pltpu.VMEM((L, RET_DV), F32)],
        compiler_params=_cp(2, 40),
        name="retention_ctx" if ctx else "retention_lat",
    )(*args)


def _gelu_tanh(x):
    cdf = 0.5 * (1.0 + jnp.tanh(math.sqrt(2.0 / math.pi) * (x + 0.044715 * (x * x * x))))
    return x * cdf


def _s5_kernel(u_ref, wb_ref, wc_ref, a_ref, d_ref, *rest, L, nseq, has_h0, write_state):
    rest = list(rest)
    h0_ref = rest.pop(0) if has_h0 else None
    y_ref = rest.pop(0)
    st_ref = rest.pop(0) if write_state else None
    bu, yacc = rest
    ns = S5_NS
    tch = S5_CHUNK_ROWS // nseq
    nch = L // tch
    crows = S5_CHUNK_ROWS

    def u_rows(ch):
        return u_ref[pl.ds(ch * crows, crows), :]

    for d in (0, 1):
        a_re = jnp.broadcast_to(a_ref[d, :, 0:ns], (nseq, ns))
        a_im = jnp.broadcast_to(a_ref[d, :, ns:2 * ns], (nseq, ns))
        if has_h0:
            st = (h0_ref[d, :, 0:ns], h0_ref[d, :, ns:2 * ns])
        else:
            st = (jnp.zeros((nseq, ns), F32), jnp.zeros((nseq, ns), F32))
        for ch in (range(nch) if d == 0 else reversed(range(nch))):
            bu[...] = jnp.dot(u_rows(ch).astype(BF16), wb_ref[d], preferred_element_type=F32)

            def step(s, carry, d=d, a_re=a_re, a_im=a_im):
                t = s if d == 0 else tch - 1 - s
                r = pl.ds(pl.multiple_of(t * nseq, nseq), nseq)
                sr, si = carry
                nr = a_re * sr - a_im * si + bu[r, 0:ns]
                ni = a_re * si + a_im * sr + bu[r, ns:2 * ns]
                bu[r, 0:ns] = nr
                bu[r, ns:2 * ns] = ni
                return nr, ni

            st = lax.fori_loop(0, tch, step, st, unroll=4)
            yc = jnp.dot(bu[...].astype(BF16), wc_ref[d], preferred_element_type=F32)
            rows = pl.ds(ch * crows, crows)
            if d == 0:
                yacc[rows, :] = yc
            else:
                yacc[rows, :] += yc
        if write_state:
            st_ref[d, :, 0:ns] = st[0]
            st_ref[d, :, ns:2 * ns] = st[1]
    for ch in range(nch):
        rows = pl.ds(ch * crows, crows)
        y = yacc[rows, :] + d_ref[...] * u_rows(ch)
        y_ref[rows, :] = _gelu_tanh(y).astype(y_ref.dtype)


def s5_scan(u, wb, wc, a_cat, d_skip, h0, *, L, nseq, nbt, write_state):
    has_h0 = h0 is not None
    st_spec = pl.BlockSpec((2, None, None, nseq, 2 * S5_NS), lambda g, b: (0, g, b, 0, 0))
    in_specs = [pl.BlockSpec((None, L * nseq, 128), lambda g, b: (b, 0, g)),
                pl.BlockSpec((2, None, 128, 2 * S5_NS), lambda g, b: (0, g, 0, 0)),
                pl.BlockSpec((2, None, 2 * S5_NS, 128), lambda g, b: (0, g, 0, 0)),
                pl.BlockSpec((2, None, 1, 2 * S5_NS), lambda g, b: (0, g, 0, 0)),
                pl.BlockSpec((1, 128), lambda g, b: (0, g))]
    args = [u, wb, wc, a_cat, d_skip.reshape(1, D)]
    if has_h0:
        in_specs.append(st_spec)
        args.append(h0)
    out_shape = [jax.ShapeDtypeStruct((nbt, L * nseq, D), BF16)]
    out_specs = [pl.BlockSpec((None, L * nseq, 128), lambda g, b: (b, 0, g))]
    if write_state:
        out_shape.append(jax.ShapeDtypeStruct((2, S5_GB, nbt, nseq, 2 * S5_NS), F32))
        out_specs.append(st_spec)
    return pl.pallas_call(
        functools.partial(_s5_kernel, L=L, nseq=nseq, has_h0=has_h0, write_state=write_state),
        out_shape=tuple(out_shape),
        grid=(S5_GB, nbt),
        in_specs=in_specs,
        out_specs=tuple(out_specs),
        scratch_shapes=[pltpu.VMEM((S5_CHUNK_ROWS, 2 * S5_NS), F32), pltpu.VMEM((L * nseq, 128), F32)],
        compiler_params=_cp(2, 48),
        name="s5_ctx" if write_state else "s5_lat",
    )(*args)


def s5_params(lam_re, lam_im, log_dt, b_re, b_im, c_re, c_im):
    dt = jnp.exp(log_dt)[..., None]
    dl_re, dl_im = lam_re * dt, lam_im * dt
    mag = jnp.exp(dl_re)
    a_re, a_im = mag * jnp.cos(dl_im), mag * jnp.sin(dl_im)
    den = lam_re * lam_re + lam_im * lam_im
    z_re = a_re - 1.0
    coef_re = (z_re * lam_re + a_im * lam_im) / den
    coef_im = (a_im * lam_re - z_re * lam_im) / den
    bb_re = coef_re[..., None] * b_re - coef_im[..., None] * b_im
    bb_im = coef_re[..., None] * b_im + coef_im[..., None] * b_re
    eye = jnp.eye(8, dtype=F32)

    def pack_b(bb):
        t = bb.reshape(2, S5_GB, 8, 64, 16).transpose(0, 1, 2, 4, 3)
        return (t[:, :, :, :, None, :] * eye[None, None, :, None, :, None]).reshape(2, S5_GB, 128, S5_NS)

    def pack_c(cc):
        t = cc.reshape(2, S5_GB, 8, 16, 64).transpose(0, 1, 2, 4, 3)
        return (t[:, :, :, :, None, :] * eye[None, None, :, None, :, None]).reshape(2, S5_GB, S5_NS, 128)

    wb = jnp.concatenate([pack_b(bb_re), pack_b(bb_im)], axis=-1).astype(BF16)
    wc = jnp.concatenate([pack_c(c_re), -pack_c(c_im)], axis=2).astype(BF16)
    a_cat = jnp.concatenate([a_re.reshape(2, S5_GB, S5_NS), a_im.reshape(2, S5_GB, S5_NS)], axis=-1)
    return wb, wc, a_cat.reshape(2, S5_GB, 1, 2 * S5_NS)


def _rope(x, cos, sin_signed):
    lane = lax.broadcasted_iota(jnp.int32, x.shape, 1)
    first_half = (lane % 64) < 32
    rot = jnp.where(first_half, pltpu.roll(x, 96, 1), pltpu.roll(x, 32, 1))
    return x * cos + rot * sin_signed


def _attn_kernel(lam_ref, q_ref, kn_ref, vn_ref, sub_ref, *rest, n_cache, rope, lam_init):
    rest = list(rest)
    if n_cache:
        ck_ref, cv_ref = rest.pop(0), rest.pop(0)
    if rope:
        cosq, sinq, cosk, sink = (rest.pop(0) for _ in range(4))
    o_ref, kall, vall = rest
    dh = DA_DH

    @pl.when(pl.program_id(2) == 0)
    def _():
        if n_cache:
            kall[0:n_cache, :] = ck_ref[...].astype(BF16)
            vall[0:n_cache, :] = cv_ref[...].astype(BF16)
        kn = kn_ref[...]
        if rope:
            kn = jnp.concatenate([_rope(kn[:, m * dh:(m + 1) * dh], cosk[...], sink[...]) for m in (0, 1)],
                                 axis=1)
        kall[n_cache:, :] = kn.astype(BF16)
        vall[n_cache:, :] = vn_ref[...].astype(BF16)

    q = q_ref[...]
    nt = (((1,), (1,)), ((), ()))
    probs = []
    for m in (0, 1):
        qm = q[:, m * dh:(m + 1) * dh]
        if rope:
            qm = _rope(qm, cosq[...], sinq[...])
        s = lax.dot_general(qm.astype(BF16), kall[:, m * dh:(m + 1) * dh], nt,
                            preferred_element_type=F32) * (dh ** -0.5)
        e = jnp.exp(s - jnp.max(s, axis=-1, keepdims=True))
        probs.append(e / jnp.sum(e, axis=-1, keepdims=True))
    p = (probs[0] - lam_ref[0] * probs[1]).astype(BF16)
    o = jnp.dot(p, vall[...], preferred_element_type=F32)
    of = o * lax.rsqrt(jnp.mean(o * o, axis=-1, keepdims=True) + EPS) * (1.0 - lam_init)
    o_ref[...] = (of * sub_ref[...]).astype(o_ref.dtype)


def diff_attention(qkv, subln, lam, lam_init, cache_k, cache_v, rope_tabs, *, ctx):
    tq = 256
    nb, L, blk0 = (CTX_B, CTX_L, 0) if ctx else (LAT_B, LAT_L, N_CTX // LAT_L)
    nq = L // tq
    qblk0 = blk0 * (LAT_L // tq) if not ctx else 0
    n_cache = 0 if ctx else PAST
    in_specs = [pl.BlockSpec(memory_space=pltpu.SMEM),
                pl.BlockSpec((tq, DA_VD), lambda b, h, qi: (qblk0 + b * nq + qi, h)),
                pl.BlockSpec((L, DA_VD), lambda b, h, qi: (blk0 + b, DA_H + h)),
                pl.BlockSpec((L, DA_VD), lambda b, h, qi: (blk0 + b, 2 * DA_H + h)),
                pl.BlockSpec((1, DA_VD), lambda b, h, qi: (0, h))]
    args = [lam, qkv, qkv, qkv, subln.reshape(1, D)]
    if not ctx:
        cache_spec = pl.BlockSpec((None, PAST, DA_VD), lambda b, h, qi: (b, 0, h))
        in_specs += [cache_spec, cache_spec]
        args += [cache_k.reshape(LAT_B, PAST, D), cache_v.reshape(LAT_B, PAST, D)]
        cos, sin_signed = rope_tabs
        in_specs += [pl.BlockSpec((tq, DA_DH), lambda b, h, qi: (qi, 0))] * 2
        in_specs += [pl.BlockSpec((L, DA_DH), lambda b, h, qi: (0, 0))] * 2
        args += [cos, sin_signed, cos, sin_signed]
    lk = n_cache + L
    return pl.pallas_call(
        functools.partial(_attn_kernel, n_cache=n_cache, rope=not ctx, lam_init=lam_init),
        out_shape=jax.ShapeDtypeStruct((nb * L, D), BF16),
        grid=(nb, DA_H, nq),
        in_specs=in_specs,
        out_specs=pl.BlockSpec((tq, DA_VD), lambda b, h, qi: (b * nq + qi, h)),
        scratch_shapes=[pltpu.VMEM((lk, 2 * DA_DH), BF16), pltpu.VMEM((lk, DA_VD), BF16)],
        compiler_params=_cp(3, 40),
        name="diff_attn_ctx" if ctx else "diff_attn_lat",
    )(*args)


def rope_tables(L):
    rows = L // GRID_W
    row = jnp.repeat(jnp.arange(rows, dtype=F32), GRID_W)
    col = jnp.tile(jnp.arange(GRID_W, dtype=F32), rows)
    n_freq = DA_DH // 4
    inv = ROPE_BASE ** (-jnp.arange(n_freq, dtype=F32) / n_freq)
    ang_r, ang_c = row[:, None] * inv, col[:, None] * inv
    cos = jnp.concatenate([jnp.cos(ang_r)] * 2 + [jnp.cos(ang_c)] * 2, axis=-1)
    sin_signed = jnp.concatenate([-jnp.sin(ang_r), jnp.sin(ang_r), -jnp.sin(ang_c), jnp.sin(ang_c)], axis=-1)
    return cos, sin_signed


SLAB = D // 128
PITCH = SLAB + 4


def _gather_rows_kernel(tok_ref, nu_ref, h_hbm, o_ref, buf, sem, *, rb):
    i = pl.program_id(0)
    n_steps_used = nu_ref[0] * (MOE_TM // rb)

    def issue(step, slot):
        base = step * rb

        def body(r, carry):
            src = pl.multiple_of(tok_ref[base + r] * SLAB, SLAB)
            pltpu.make_async_copy(h_hbm.at[pl.ds(src, SLAB), :], buf.at[slot, pl.ds(r * PITCH, SLAB), :],
                                  sem.at[slot]).start()
            return carry

        lax.fori_loop(0, rb, body, 0)

    @pl.when(i == 0)
    def _():
        issue(0, 0)

    @pl.when(i + 1 < n_steps_used)
    def _():
        issue(i + 1, (i + 1) % 2)

    @pl.when(i < n_steps_used)
    def _():
        slot = i % 2
        pltpu.make_async_copy(h_hbm.at[pl.ds(0, rb * SLAB), :], buf.at[slot, pl.ds(0, rb * SLAB), :],
                              sem.at[slot]).wait()
        for c in range(SLAB):
            o_ref[:, c * 128:(c + 1) * 128] = buf[slot, pl.ds(c, rb, stride=PITCH), :].astype(o_ref.dtype)

    @pl.when(i >= n_steps_used)
    def _():
        o_ref[...] = jnp.zeros(o_ref.shape, o_ref.dtype)


def gather_rows(h_slabs, row_tok, n_used):
    rb = GATHER_RB
    return pl.pallas_call(
        functools.partial(_gather_rows_kernel, rb=rb),
        out_shape=jax.ShapeDtypeStruct((MOE_ROWS, D), BF16),
        grid_spec=pltpu.PrefetchScalarGridSpec(
            num_scalar_prefetch=2,
            grid=(MOE_ROWS // rb,),
            in_specs=[pl.BlockSpec(memory_space=pl.ANY)],
            out_specs=pl.BlockSpec((rb, D), lambda i, tok, nu: (i, 0)),
            scratch_shapes=[pltpu.VMEM((2, rb * PITCH, 128), F32), pltpu.SemaphoreType.DMA((2,))]),
        compiler_params=_cp(1, 40),
        name="moe_gather",
    )(row_tok, n_used, h_slabs)


def _ffn_up_kernel(be_ref, first_ref, nu_ref, x_ref, wg_ref, wu_ref, a_ref, wbf, *, tf):
    i = pl.program_id(1)

    @pl.when(first_ref[i] == 1)
    def _():
        wbf[:, 0:tf] = wg_ref[...].astype(BF16)
        wbf[:, tf:2 * tf] = wu_ref[...].astype(BF16)

    @pl.when(i < nu_ref[0])
    def _():
        gu = jnp.dot(x_ref[...], wbf[...], preferred_element_type=F32)
        g, u = gu[:, 0:tf], gu[:, tf:2 * tf]
        a_ref[...] = ((g * jax.nn.sigmoid(g)) * u).astype(a_ref.dtype)

    @pl.when(i >= nu_ref[0])
    def _():
        a_ref[...] = jnp.zeros(a_ref.shape, a_ref.dtype)


def _ffn_down_kernel(be_ref, first_ref, nu_ref, a_ref, w2_ref, y_ref, wbf):
    i = pl.program_id(1)

    @pl.when(first_ref[i] == 1)
    def _():
        wbf[...] = w2_ref[...].astype(BF16)

    @pl.when(i < nu_ref[0])
    def _():
        y_ref[...] = jnp.dot(a_ref[...], wbf[...], preferred_element_type=F32)

    @pl.when(i >= nu_ref[0])
    def _():
        y_ref[...] = jnp.zeros(y_ref.shape, y_ref.dtype)


def expert_ffn(xs, w13_all, w2_all, layer, blk_e, blk_first, n_used):
    tm, tf, tn = MOE_TM, MOE_TF, MOE_TN
    nj = D_EXP // tf

    def rows(i, nu):
        return jnp.minimum(i, nu[0] - 1)

    act = pl.pallas_call(
        functools.partial(_ffn_up_kernel, tf=tf),
        out_shape=jax.ShapeDtypeStruct((MOE_ROWS, D_EXP), BF16),
        grid_spec=pltpu.PrefetchScalarGridSpec(
            num_scalar_prefetch=3,
            grid=(nj, MOE_NB),
            in_specs=[pl.BlockSpec((tm, D), lambda j, i, be, fi, nu: (rows(i, nu), 0)),
                      pl.BlockSpec((None, None, D, tf), lambda j, i, be, fi, nu: (layer, be[i], 0, j)),
                      pl.BlockSpec((None, None, D, tf), lambda j, i, be, fi, nu: (layer, be[i], 0, nj + j))],
            out_specs=pl.BlockSpec((tm, tf), lambda j, i, be, fi, nu: (i, j)),
            scratch_shapes=[pltpu.VMEM((D, 2 * tf), BF16)]),
        compiler_params=_cp(2, 44),
        name="moe_ffn_up",
    )(blk_e, blk_first, n_used, xs, w13_all, w13_all)
    return pl.pallas_call(
        _ffn_down_kernel,
        out_shape=jax.ShapeDtypeStruct((MOE_ROWS, D), F32),
        grid_spec=pltpu.PrefetchScalarGridSpec(
            num_scalar_prefetch=3,
            grid=(D // tn, MOE_NB),
            in_specs=[pl.BlockSpec((tm, D_EXP), lambda n, i, be, fi, nu: (rows(i, nu), 0)),
                      pl.BlockSpec((None, None, D_EXP, tn), lambda n, i, be, fi, nu: (layer, be[i], 0, n))],
            out_specs=pl.BlockSpec((tm, tn), lambda n, i, be, fi, nu: (i, n)),
            scratch_shapes=[pltpu.VMEM((D_EXP, tn), BF16)]),
        compiler_params=_cp(2, 44),
        name="moe_ffn_down",
    )(blk_e, blk_first, n_used, act, w2_all)


def _combine_kernel(pos_ref, x_ref, gt_ref, gate_ref, fw_ref, ys_hbm, o_ref, buf, sem, *, tc, nsteps, final):
    i = pl.program_id(0)

    def issue(step, slot):
        base = step * tc

        def body(r, carry):
            for kk in (0, 1):
                p = pos_ref[kk * N_TOK + base + r]
                pltpu.make_async_copy(ys_hbm.at[pl.ds(p, 1), :], buf.at[slot, kk, pl.ds(r, 1), :],
                                      sem.at[slot]).start()
            return carry

        lax.fori_loop(0, tc, body, 0)

    @pl.when(i == 0)
    def _():
        issue(0, 0)

    @pl.when(i + 1 < nsteps)
    def _():
        issue(i + 1, (i + 1) % 2)

    slot = i % 2
    for kk in (0, 1):
        pltpu.make_async_copy(ys_hbm.at[pl.ds(0, tc), :], buf.at[slot, kk], sem.at[slot]).wait()
    gt = gt_ref[...]
    moe = gt[:, 0:1] * buf[slot, 0] + gt[:, 1:2] * buf[slot, 1]
    x2 = x_ref[...] + gate_ref[...] * moe
    if final:
        x2 = x2 * lax.rsqrt(jnp.mean(x2 * x2, axis=-1, keepdims=True) + EPS) * fw_ref[...]
    o_ref[...] = x2


def moe_combine(x, ys, pos, gates_t, mod, final_w, *, final):
    tc = COMBINE_TC
    nsteps = N_TOK // tc
    return pl.pallas_call(
        functools.partial(_combine_kernel, tc=tc, nsteps=nsteps, final=final),
        out_shape=jax.ShapeDtypeStruct((N_TOK, D), F32),
        grid_spec=pltpu.PrefetchScalarGridSpec(
            num_scalar_prefetch=1,
            grid=(nsteps,),
            in_specs=[pl.BlockSpec((tc, D), lambda i, pos: (i, 0)),
                      pl.BlockSpec((tc, 2), lambda i, pos: (i, 0)),
                      _mod_spec(tc, J_GATE2),
                      pl.BlockSpec((1, D), lambda i, pos: (0, 0)),
                      pl.BlockSpec(memory_space=pl.ANY)],
            out_specs=pl.BlockSpec((tc, D), lambda i, pos: (i, 0)),
            scratch_shapes=[pltpu.VMEM((2, 2, tc, D), F32), pltpu.SemaphoreType.DMA((2,))]),
        compiler_params=_cp(1, 40),
        name="moe_combine",
    )(pos, x, gates_t, mod, final_w.reshape(1, D), ys)


def dispatch_plan(ids, rank, tile_cnt):
    tile_off = jnp.cumsum(tile_cnt, axis=0) - tile_cnt
    counts = jnp.sum(tile_cnt, axis=0)
    nblk = (counts + MOE_TM - 1) // MOE_TM
    blk_end = jnp.cumsum(nblk)
    n_used = blk_end[-1]
    tile = (jnp.arange(N_TOK, dtype=jnp.int32) // ROUTE_TM)[None, :]
    pos = ((blk_end - nblk)[ids] * MOE_TM + tile_off.reshape(-1)[tile * N_EXP + ids] + rank).astype(jnp.int32)
    tok = jnp.tile(jnp.arange(N_TOK, dtype=jnp.int32), 2)
    row_tok = (jnp.arange(MOE_ROWS, dtype=jnp.int32) % N_TOK).at[pos.reshape(-1)].set(tok)
    blk = jnp.minimum(jnp.arange(MOE_NB, dtype=jnp.int32), n_used - 1)
    blk_e = jnp.minimum(jnp.searchsorted(blk_end, blk, side="right"), N_EXP - 1).astype(jnp.int32)
    blk_first = jnp.concatenate([jnp.ones((1,), jnp.int32), (blk_e[1:] != blk_e[:-1]).astype(jnp.int32)])
    return pos.reshape(-1), row_tok, blk_e, blk_first, n_used.reshape(1).astype(jnp.int32)


def moe_layer(x, norm_g, mod, router_wt, router_b, w13_all, w2_all, layer, final_w, *, final):
    h2, ids, gates, rank, tile_cnt = normmod_route(x, norm_g, mod, router_wt, router_b)
    pos, row_tok, blk_e, blk_first, n_used = dispatch_plan(ids, rank, tile_cnt[:, :, 0])
    xs = gather_rows(h2.reshape(N_TOK * SLAB, 128), row_tok, n_used)
    ys = expert_ffn(xs, w13_all, w2_all, layer, blk_e, blk_first, n_used)
    return moe_combine(x, ys, pos, gates.T, mod, final_w, final=final)


def _to_time_major(h):
    hc = h[:N_CTX].reshape(CTX_B // 8, 8, CTX_L, D).transpose(0, 2, 1, 3).reshape(CTX_B // 8, CTX_L * 8, D)
    hl = h[N_CTX:].reshape(LAT_B, LAT_L, D).transpose(1, 0, 2)
    hl = jnp.pad(hl, ((0, 0), (0, 8 - LAT_B), (0, 0))).reshape(1, LAT_L * 8, D)
    return hc, hl


def _from_time_major(yc, yl):
    yc = yc.reshape(CTX_B // 8, CTX_L, 8, D).transpose(0, 2, 1, 3).reshape(N_CTX, D)
    yl = yl.reshape(LAT_L, 8, D)[:, :LAT_B].transpose(1, 0, 2).reshape(N_LAT, D)
    return jnp.concatenate([yc, yl], axis=0)


def kernel(x_prompt, x_sample, state_ret_l0, state_s5_l1, cache_k_l2, cache_v_l2, state_ret_l3, c, c_ctx, ada_w, ada_b, norm_w, final_norm_w, router_w, router_b, moe_w13, moe_w2, l0_ret_w_in, l0_ret_w_o, l0_ret_gn, l0_ret_decay, l1_s5_lam_re, l1_s5_lam_im, l1_s5_log_dt, l1_s5_b_re, l1_s5_b_im, l1_s5_c_re, l1_s5_c_im, l1_s5_d, l1_s5_w_glu, l1_s5_b_glu, l2_da_w_qkv, l2_da_w_o, l2_da_subln, l2_da_lambda, l3_ret_w_in, l3_ret_w_o, l3_ret_gn, l3_ret_decay):
    x = jnp.concatenate([x_prompt.reshape(N_CTX, D), x_sample.reshape(N_LAT, D)], axis=0)
    cvecs = jnp.concatenate([c_ctx[None], c, jnp.zeros((MOD_ROWS - 1 - LAT_B, D), F32)], axis=0)
    mods = modulation_all(cvecs, ada_w, ada_b)
    router_wt = router_w.T

    def retention_layer(i, x, w_in, w_o, gn, decay, s0, proj_dtype):
        mod = mods[i]
        h = normmod(x, norm_w[i, 0], mod, J_SHIFT1, J_SCALE1, BF16)
        qkvg = matmul(h, w_in, proj_dtype)
        log_gamma = -jnp.exp(decay)
        o_c, st = retention(qkvg, gn, log_gamma, None, ctx=True)
        (o_l,) = retention(qkvg, gn, log_gamma, s0, ctx=False)
        return matmul_residual(o_c, o_l, w_o.astype(BF16), x, mod, J_GATE1, tm=512, tn=256), st

    def moe(i, x):
        return moe_layer(x, norm_w[i, 1], mods[i], router_wt, router_b, moe_w13, moe_w2, i, final_norm_w,
                         final=(i == DEPTH - 1))

    x, st_ret0 = retention_layer(0, x, l0_ret_w_in, l0_ret_w_o, l0_ret_gn, l0_ret_decay, state_ret_l0, F32)
    x = moe(0, x)

    mod = mods[1]
    h = normmod(x, norm_w[1, 0], mod, J_SHIFT1, J_SCALE1, F32)
    wb, wc, a_cat = s5_params(l1_s5_lam_re, l1_s5_lam_im, l1_s5_log_dt, l1_s5_b_re, l1_s5_b_im,
                              l1_s5_c_re, l1_s5_c_im)
    u_c, u_l = _to_time_major(h)
    h0 = state_s5_l1.reshape(LAT_B, 2, S5_GB, 8, 64, 2).transpose(1, 2, 0, 5, 3, 4)
    h0 = jnp.pad(h0.reshape(2, S5_GB, LAT_B, 2 * S5_NS), ((0, 0), (0, 0), (0, 8 - LAT_B), (0, 0)))
    h0 = h0.reshape(2, S5_GB, 1, 8, 2 * S5_NS)
    y_c, st5 = s5_scan(u_c, wb, wc, a_cat, l1_s5_d, None, L=CTX_L, nseq=8, nbt=CTX_B // 8, write_state=True)
    (y_l,) = s5_scan(u_l, wb, wc, a_cat, l1_s5_d, h0, L=LAT_L, nseq=8, nbt=1, write_state=False)
    y = _from_time_major(y_c, y_l)
    x = matmul_glu_residual(y, l1_s5_w_glu, l1_s5_b_glu, x, mod, J_GATE1)
    st_s5 = st5.reshape(2, S5_GB, CTX_B // 8, 8, 2, 8, 64).transpose(2, 3, 0, 1, 5, 6, 4)
    st_s5 = st_s5.reshape(CTX_B, 2, S5_GB * 8, 64, 2)
    x = moe(1, x)

    mod = mods[2]
    lam_init = 0.8 - 0.6 * math.exp(-0.3 * 2)
    lp = l2_da_lambda
    lam = (jnp.exp(jnp.sum(lp[0] * lp[1])) - jnp.exp(jnp.sum(lp[2] * lp[3])) + lam_init).reshape(1)
    h = normmod(x, norm_w[2, 0], mod, J_SHIFT1, J_SCALE1, BF16)
    qkv = matmul(h, l2_da_w_qkv, F32)
    a_c = diff_attention(qkv, l2_da_subln, lam, lam_init, None, None, None, ctx=True)
    a_l = diff_attention(qkv, l2_da_subln, lam, lam_init, cache_k_l2, cache_v_l2, rope_tables(LAT_L), ctx=False)
    x = matmul_residual(a_c, a_l, l2_da_w_o.astype(BF16), x, mod, J_GATE1)
    new_k = qkv[:N_CTX, D:2 * D].reshape(CTX_B, CTX_L, DA_H, 2, DA_DH)
    new_v = qkv[:N_CTX, 2 * D:].reshape(CTX_B, CTX_L, DA_H, DA_VD)
    x = moe(2, x)

    x, st_ret3 = retention_layer(3, x, l3_ret_w_in, l3_ret_w_o, l3_ret_gn, l3_ret_decay, state_ret_l3, BF16)
    y = moe(3, x)

    y_prompt = y[:N_CTX].reshape(CTX_B, CTX_L, D)
    y_sample = y[N_CTX:].reshape(LAT_B, LAT_L, D)
    return (y_prompt, y_sample, st_ret0, st_s5, new_k, new_v, st_ret3)
```

```python
import functools
import math

import jax
import jax.numpy as jnp
from jax import lax
from jax.experimental import pallas as pl
from jax.experimental.pallas import tpu as pltpu

F32 = jnp.float32
BF16 = jnp.bfloat16

D = 4096
CTX_B, CTX_L = 16, 256
LAT_B, LAT_L = 4, 1024
PAST = 512
N_CTX = CTX_B * CTX_L
N_LAT = LAT_B * LAT_L
N_TOK = N_CTX + N_LAT
DEPTH = 4
EPS = 1e-6

RET_H, RET_DK, RET_DV = 16, 256, 512
RET_C = 128
S5_GB = 32
S5_NS = 512
S5_CHUNK_ROWS = 2048
DA_H, DA_DH, DA_VD = 16, 128, 256
N_EXP, D_EXP = 16, 1024
GRID_W = 64
ROPE_BASE = 10000.0

MOE_TM = 512
MOE_TF = 256
MOE_TN = 2048
MOE_NB = 2 * N_TOK // MOE_TM + N_EXP
MOE_ROWS = MOE_NB * MOE_TM
GATHER_RB = 256
SLAB = D // 128
PITCH = SLAB + 4
COMBINE_TC = 128

MIB = 1024 * 1024
MOD_ROWS = 8
J_SHIFT1, J_SCALE1, J_GATE1, J_SHIFT2, J_SCALE2, J_GATE2 = range(6)


def _cp(n_axes, vmem_mib):
    return pltpu.CompilerParams(dimension_semantics=("arbitrary",) * n_axes,
                                vmem_limit_bytes=vmem_mib * MIB)


def _mod_row(i, tm):
    t0 = i * tm
    return jnp.where(t0 < N_CTX, 0, t0 // LAT_L - (N_CTX // LAT_L - 1))


def _mod_spec(tm, j, tn=D, col=None):
    if col is None:
        return pl.BlockSpec((None, None, 1, tn), lambda i, *_: (_mod_row(i, tm), j, 0, 0))
    return pl.BlockSpec((None, None, 1, tn), lambda i, *a: (_mod_row(i, tm), j, 0, col(i, *a)))


def _mod_kernel(c_ref, w_ref, b_ref, o_ref):
    c = c_ref[...]
    s = (c * jax.nn.sigmoid(c)).astype(BF16)
    o_ref[...] = jnp.dot(s, w_ref[...].astype(BF16), preferred_element_type=F32) + b_ref[...]


def modulation_all(cvecs, ada_w, ada_b):
    tn = 512
    n = 6 * D
    out = pl.pallas_call(
        _mod_kernel,
        out_shape=jax.ShapeDtypeStruct((DEPTH, MOD_ROWS, n), F32),
        grid=(DEPTH, n // tn),
        in_specs=[pl.BlockSpec((MOD_ROWS, D), lambda l, j: (0, 0)),
                  pl.BlockSpec((None, D, tn), lambda l, j: (l, 0, j)),
                  pl.BlockSpec((None, 1, tn), lambda l, j: (l, 0, j))],
        out_specs=pl.BlockSpec((None, MOD_ROWS, tn), lambda l, j: (l, 0, j)),
        compiler_params=_cp(2, 40),
        name="modulation",
    )(cvecs, ada_w, ada_b.reshape(DEPTH, 1, n))
    return out.reshape(DEPTH, MOD_ROWS, 6, 1, D)


def _normmod(x, g_ref, sh_ref, sc_ref):
    y = x * lax.rsqrt(jnp.mean(x * x, axis=-1, keepdims=True) + EPS)
    return y * g_ref[...] * (1.0 + sc_ref[...]) + sh_ref[...]


def _normmod_kernel(x_ref, g_ref, sh_ref, sc_ref, o_ref):
    o_ref[...] = _normmod(x_ref[...], g_ref, sh_ref, sc_ref).astype(o_ref.dtype)


def normmod(x, g, mod, j_shift, j_scale, out_dtype):
    tm = 256
    return pl.pallas_call(
        _normmod_kernel,
        out_shape=jax.ShapeDtypeStruct((N_TOK, D), out_dtype),
        grid=(N_TOK // tm,),
        in_specs=[pl.BlockSpec((tm, D), lambda i: (i, 0)),
                  pl.BlockSpec((1, D), lambda i: (0, 0)),
                  _mod_spec(tm, j_shift), _mod_spec(tm, j_scale)],
        out_specs=pl.BlockSpec((tm, D), lambda i: (i, 0)),
        compiler_params=_cp(1, 40),
        name="normmod",
    )(x, g.reshape(1, D), mod, mod)


def _row(a, r):
    return a[r:r + 1, :]


def _first_argmax(vals):
    best, idx = vals[0], jnp.zeros(vals[0].shape, jnp.int32)
    for j in range(1, len(vals)):
        better = vals[j] > best
        idx = jnp.where(better, j, idx)
        best = jnp.where(better, vals[j], best)
    return idx, best


def _select(vals, idx):
    out = vals[0]
    for j in range(1, len(vals)):
        out = jnp.where(idx == j, vals[j], out)
    return out


def _normmod_route_kernel(x_ref, g_ref, sh_ref, sc_ref, rwt_ref, rb_ref, h_ref, ids_ref, gates_ref,
                          rank_ref, cnt_ref):
    h = _normmod(x_ref[...], g_ref, sh_ref, sc_ref)
    n_tok_tile = h.shape[0]
    for c in range(SLAB):
        h_ref[pl.ds(c, n_tok_tile, stride=PITCH), :] = h[:, c * 128:(c + 1) * 128]
    for c in range(SLAB, PITCH):
        h_ref[pl.ds(c, n_tok_tile, stride=PITCH), :] = jnp.zeros((n_tok_tile, 128), F32)
    logits = lax.dot_general(rwt_ref[...].astype(BF16), h.astype(BF16), (((1,), (1,)), ((), ())),
                             preferred_element_type=F32)
    ex = jnp.exp(logits - jnp.max(logits, axis=0, keepdims=True))
    probs = ex / jnp.sum(ex, axis=0, keepdims=True)
    sel = probs + rb_ref[...]
    n_grp, epg = 4, 4
    scores = []
    for g in range(n_grp):
        a, b, c, d = (_row(sel, epg * g + j) for j in range(epg))
        m1, n1 = jnp.maximum(a, b), jnp.minimum(a, b)
        m2, n2 = jnp.maximum(c, d), jnp.minimum(c, d)
        scores.append(jnp.maximum(m1, m2) + jnp.maximum(jnp.minimum(m1, m2), jnp.maximum(n1, n2)))
    grp, _ = _first_argmax(scores)
    v = [_select([_row(sel, epg * g + j) for g in range(n_grp)], grp) for j in range(epg)]
    p = [_select([_row(probs, epg * g + j) for g in range(n_grp)], grp) for j in range(epg)]
    l1, _ = _first_argmax(v)
    neg = jnp.full(v[0].shape, -jnp.inf, F32)
    l2, _ = _first_argmax([jnp.where(l1 == j, neg, v[j]) for j in range(epg)])
    w1, w2 = _select(p, l1), _select(p, l2)
    tot = w1 + w2
    id1, id2 = grp * epg + l1, grp * epg + l2
    ids_ref[0:1, :] = id1
    ids_ref[1:2, :] = id2
    gates_ref[0:1, :] = w1 / tot
    gates_ref[1:2, :] = w2 / tot
    tm = id1.shape[1]
    eid = lax.broadcasted_iota(jnp.int32, (N_EXP, tm), 0)
    pick1, pick2 = eid == id1, eid == id2
    onehot = jnp.where(pick1, 1.0, 0.0) + jnp.where(pick2, 1.0, 0.0)
    before = lax.broadcasted_iota(jnp.int32, (tm, tm), 0) < lax.broadcasted_iota(jnp.int32, (tm, tm), 1)
    tri = jnp.where(before, 1.0, 0.0).astype(BF16)
    prefix = jnp.dot(onehot.astype(BF16), tri, preferred_element_type=F32)
    rank_ref[0:1, :] = jnp.sum(jnp.where(pick1, prefix, 0.0), axis=0, keepdims=True).astype(jnp.int32)
    rank_ref[1:2, :] = jnp.sum(jnp.where(pick2, prefix, 0.0), axis=0, keepdims=True).astype(jnp.int32)
    cnt_ref[...] = jnp.sum(onehot, axis=1, keepdims=True).astype(jnp.int32)


ROUTE_TM = 256


def normmod_route(x, g, mod, router_wt, router_b):
    tm = ROUTE_TM
    nt = N_TOK // tm
    return pl.pallas_call(
        _normmod_route_kernel,
        out_shape=(jax.ShapeDtypeStruct((N_TOK * PITCH, 128), F32),
                   jax.ShapeDtypeStruct((2, N_TOK), jnp.int32),
                   jax.ShapeDtypeStruct((2, N_TOK), F32),
                   jax.ShapeDtypeStruct((2, N_TOK), jnp.int32),
                   jax.ShapeDtypeStruct((nt, N_EXP, 1), jnp.int32)),
        grid=(nt,),
        in_specs=[pl.BlockSpec((tm, D), lambda i: (i, 0)),
                  pl.BlockSpec((1, D), lambda i: (0, 0)),
                  _mod_spec(tm, J_SHIFT2), _mod_spec(tm, J_SCALE2),
                  pl.BlockSpec((N_EXP, D), lambda i: (0, 0)),
                  pl.BlockSpec((N_EXP, 1), lambda i: (0, 0))],
        out_specs=(pl.BlockSpec((tm * PITCH, 128), lambda i: (i, 0)),
                   pl.BlockSpec((2, tm), lambda i: (0, i)),
                   pl.BlockSpec((2, tm), lambda i: (0, i)),
                   pl.BlockSpec((2, tm), lambda i: (0, i)),
                   pl.BlockSpec((None, N_EXP, 1), lambda i: (i, 0, 0))),
        compiler_params=_cp(1, 40),
        name="normmod_route",
    )(x, g.reshape(1, D), mod, mod, router_wt, router_b.reshape(N_EXP, 1))


def _mm_kernel(h_ref, w_ref, o_ref):
    o_ref[...] = jnp.dot(h_ref[...], w_ref[...].astype(BF16),
                         preferred_element_type=F32).astype(o_ref.dtype)


def matmul(h, w, out_dtype, tm=1024, tn=512):
    m, k = h.shape
    n = w.shape[1]
    return pl.pallas_call(
        _mm_kernel,
        out_shape=jax.ShapeDtypeStruct((m, n), out_dtype),
        grid=(m // tm, n // tn),
        in_specs=[pl.BlockSpec((tm, k), lambda i, j: (i, 0)),
                  pl.BlockSpec((k, tn), lambda i, j: (0, j))],
        out_specs=pl.BlockSpec((tm, tn), lambda i, j: (i, j)),
        compiler_params=_cp(2, 52),
        name="matmul",
    )(h, w)


def _mm_res_kernel(hc_ref, hl_ref, w_ref, x_ref, gate_ref, o_ref, *, n_ctx_tiles):
    def run(h_ref):
        y = jnp.dot(h_ref[...], w_ref[...].astype(BF16), preferred_element_type=F32)
        o_ref[...] = x_ref[...] + gate_ref[...] * y

    i = pl.program_id(0)

    @pl.when(i < n_ctx_tiles)
    def _():
        run(hc_ref)

    @pl.when(i >= n_ctx_tiles)
    def _():
        run(hl_ref)


def matmul_residual(h_ctx, h_lat, w, x, mod, j_gate, tm=1024, tn=512):
    k = h_ctx.shape[1]
    n = w.shape[1]
    nc = N_CTX // tm
    return pl.pallas_call(
        functools.partial(_mm_res_kernel, n_ctx_tiles=nc),
        out_shape=jax.ShapeDtypeStruct((N_TOK, n), F32),
        grid=(N_TOK // tm, n // tn),
        in_specs=[pl.BlockSpec((tm, k), lambda i, j: (jnp.minimum(i, nc - 1), 0)),
                  pl.BlockSpec((tm, k), lambda i, j: (jnp.maximum(i - nc, 0), 0)),
                  pl.BlockSpec((k, tn), lambda i, j: (0, j)),
                  pl.BlockSpec((tm, tn), lambda i, j: (i, j)),
                  _mod_spec(tm, j_gate, tn, lambda i, j: j)],
        out_specs=pl.BlockSpec((tm, tn), lambda i, j: (i, j)),
        compiler_params=_cp(2, 56),
        name="matmul_residual",
    )(h_ctx, h_lat, w, x, mod)


def _mm_glu_kernel(h_ref, wv_ref, wg_ref, bv_ref, bg_ref, x_ref, gate_ref, o_ref):
    h = h_ref[...]
    v = jnp.dot(h, wv_ref[...].astype(BF16), preferred_element_type=F32) + bv_ref[...]
    g = jnp.dot(h, wg_ref[...].astype(BF16), preferred_element_type=F32) + bg_ref[...]
    o_ref[...] = x_ref[...] + gate_ref[...] * (v * jax.nn.sigmoid(g))


def matmul_glu_residual(h, w, b, x, mod, j_gate, tm=1024, tn=256):
    m, k = h.shape
    n = w.shape[1] // 2
    nj = n // tn
    b2 = b.reshape(1, 2 * n)
    return pl.pallas_call(
        _mm_glu_kernel,
        out_shape=jax.ShapeDtypeStruct((m, n), F32),
        grid=(m // tm, nj),
        in_specs=[pl.BlockSpec((tm, k), lambda i, j: (i, 0)),
                  pl.BlockSpec((k, tn), lambda i, j: (0, j)),
                  pl.BlockSpec((k, tn), lambda i, j: (0, nj + j)),
                  pl.BlockSpec((1, tn), lambda i, j: (0, j)),
                  pl.BlockSpec((1, tn), lambda i, j: (0, nj + j)),
                  pl.BlockSpec((tm, tn), lambda i, j: (i, j)),
                  _mod_spec(tm, j_gate, tn, lambda i, j: j)],
        out_specs=pl.BlockSpec((tm, tn), lambda i, j: (i, j)),
        compiler_params=_cp(2, 52),
        name="matmul_glu",
    )(h, w, w, b2, b2, x, mod)


def _ret_kernel(lg_ref, q_ref, k_ref, v_ref, g_ref, gn_ref, *rest, n_chunks, has_s0, write_state):
    rest = list(rest)
    s0_ref = rest.pop(0) if has_s0 else None
    o_ref = rest.pop(0)
    st_ref = rest.pop(0) if write_state else None
    sf, sb, oacc = rest
    hd = pl.program_id(1)
    lgf, lgb = lg_ref[0, hd], lg_ref[1, hd]
    c = RET_C
    kscale = RET_DK ** -0.5
    ri = lax.broadcasted_iota(jnp.int32, (c, c), 0).astype(F32)
    ci = lax.broadcasted_iota(jnp.int32, (c, c), 1).astype(F32)
    rel = ri - ci
    dmask = (jnp.where(rel >= 0, jnp.exp(lgf * jnp.maximum(rel, 0.0)), 0.0)
             + jnp.where(rel <= 0, jnp.exp(lgb * jnp.maximum(-rel, 0.0)), 0.0)) * kscale
    pos = lax.broadcasted_iota(jnp.int32, (c, 1), 0).astype(F32)
    qd_f = jnp.exp(lgf * (pos + 1.0))
    kd_f = jnp.exp(lgf * (c - 1.0 - pos)) * kscale
    qd_b = jnp.exp(lgb * (c - pos))
    kd_b = jnp.exp(lgb * pos) * kscale
    one = jnp.ones((1, 1), F32)
    cd_f = jnp.exp(one * (lgf * c))
    cd_b = jnp.exp(one * (lgb * c))
    if has_s0:
        sf[...] = s0_ref[0]
        sb[...] = s0_ref[1]
    else:
        sf[...] = jnp.zeros(sf.shape, F32)
        sb[...] = jnp.zeros(sb.shape, F32)
    nt = (((1,), (1,)), ((), ()))

    def state_update(s_ref, q, k, v, qd, kd, cd):
        s = s_ref[...]
        o_cross = jnp.dot(q, s.astype(BF16), preferred_element_type=F32) * qd
        kdt = (k.astype(F32) * kd).T.astype(BF16)
        s_ref[...] = s * cd + jnp.dot(kdt, v, preferred_element_type=F32)
        return o_cross

    for ch in range(n_chunks):
        rows = pl.ds(ch * c, c)
        q, k, v = q_ref[rows, :].astype(BF16), k_ref[rows, :], v_ref[rows, :].astype(BF16)
        s = lax.dot_general(q, k.astype(BF16), nt, preferred_element_type=F32)
        o = jnp.dot((s * dmask).astype(BF16), v, preferred_element_type=F32)
        oacc[rows, :] = o + state_update(sf, q, k, v, qd_f, kd_f, cd_f)
    for ch in reversed(range(n_chunks)):
        rows = pl.ds(ch * c, c)
        q, k, v = q_ref[rows, :].astype(BF16), k_ref[rows, :], v_ref[rows, :].astype(BF16)
        o = oacc[rows, :] + state_update(sb, q, k, v, qd_b, kd_b, cd_b)
        mu = jnp.mean(o, axis=-1, keepdims=True)
        dev = o - mu
        var = jnp.mean(dev * dev, axis=-1, keepdims=True)
        on = dev * lax.rsqrt(var + EPS) * gn_ref[...]
        gg = g_ref[rows, :].astype(F32)
        o_ref[rows, :] = ((gg * jax.nn.sigmoid(gg)) * on).astype(o_ref.dtype)
    if write_state:
        st_ref[0] = sf[...]
        st_ref[1] = sb[...]


def retention(qkvg, gn, log_gamma, s0, *, ctx):
    nb, L, blk0 = (CTX_B, CTX_L, 0) if ctx else (LAT_B, LAT_L, N_CTX // LAT_L)
    qcols, vcols = RET_H, 2 * RET_H * RET_DK // RET_DV
    in_specs = [pl.BlockSpec(memory_space=pltpu.SMEM),
                pl.BlockSpec((L, RET_DK), lambda b, h: (blk0 + b, h)),
                pl.BlockSpec((L, RET_DK), lambda b, h: (blk0 + b, qcols + h)),
                pl.BlockSpec((L, RET_DV), lambda b, h: (blk0 + b, vcols + h)),
                pl.BlockSpec((L, RET_DV), lambda b, h: (blk0 + b, vcols + RET_H + h)),
                pl.BlockSpec((1, RET_DV), lambda b, h: (0, h))]
    args = [log_gamma, qkvg, qkvg, qkvg, qkvg, gn.reshape(1, RET_H * RET_DV)]
    out_shape = [jax.ShapeDtypeStruct((nb * L, RET_H * RET_DV), BF16)]
    out_specs = [pl.BlockSpec((L, RET_DV), lambda b, h: (b, h))]
    if ctx:
        out_shape.append(jax.ShapeDtypeStruct((nb, 2, RET_H, RET_DK, RET_DV), F32))
        out_specs.append(pl.BlockSpec((None, 2, None, RET_DK, RET_DV), lambda b, h: (b, 0, h, 0, 0)))
    else:
        in_specs.append(pl.BlockSpec((None, 2, None, RET_DK, RET_DV), lambda b, h: (b, 0, h, 0, 0)))
        args.append(s0)
    return pl.pallas_call(
        functools.partial(_ret_kernel, n_chunks=L // RET_C, has_s0=not ctx, write_state=ctx),
        out_shape=tuple(out_shape),
        grid=(nb, RET_H),
        in_specs=in_specs,
        out_specs=tuple(out_specs),
        scratch_shapes=[pltpu.VMEM((RET_DK, RET_DV), F32), pltpu.VMEM((RET_DK, RET_DV), F32),
                        pltpu.VMEM((L, RET_DV), F32)],
        compiler_params=_cp(2, 40),
        name="retention_ctx" if ctx else "retention_lat",
    )(*args)


def _gelu_tanh(x):
    cdf = 0.5 * (1.0 + jnp.tanh(math.sqrt(2.0 / math.pi) * (x + 0.044715 * (x * x * x))))
    return x * cdf


def _s5_kernel(u_ref, wb_ref, wc_ref, a_ref, d_ref, *rest, L, nseq, has_h0, write_state):
    rest = list(rest)
    h0_ref = rest.pop(0) if has_h0 else None
    y_ref = rest.pop(0)
    st_ref = rest.pop(0) if write_state else None
    bu, yacc = rest
    ns = S5_NS
    tch = S5_CHUNK_ROWS // nseq
    nch = L // tch
    crows = S5_CHUNK_ROWS

    def u_rows(ch):
        return u_ref[pl.ds(ch * crows, crows), :]

    for d in (0, 1):
        a_re = jnp.broadcast_to(a_ref[d, :, 0:ns], (nseq, ns))
        a_im = jnp.broadcast_to(a_ref[d, :, ns:2 * ns], (nseq, ns))
        if has_h0:
            st = (h0_ref[d, :, 0:ns], h0_ref[d, :, ns:2 * ns])
        else:
            st = (jnp.zeros((nseq, ns), F32), jnp.zeros((nseq, ns), F32))
        for ch in (range(nch) if d == 0 else reversed(range(nch))):
            bu[...] = jnp.dot(u_rows(ch).astype(BF16), wb_ref[d], preferred_element_type=F32)

            def step(s, carry, d=d, a_re=a_re, a_im=a_im):
                t = s if d == 0 else tch - 1 - s
                r = pl.ds(pl.multiple_of(t * nseq, nseq), nseq)
                sr, si = carry
                nr = a_re * sr - a_im * si + bu[r, 0:ns]
                ni = a_re * si + a_im * sr + bu[r, ns:2 * ns]
                bu[r, 0:ns] = nr
                bu[r, ns:2 * ns] = ni
                return nr, ni

            st = lax.fori_loop(0, tch, step, st, unroll=4)
            yc = jnp.dot(bu[...].astype(BF16), wc_ref[d], preferred_element_type=F32)
            rows = pl.ds(ch * crows, crows)
            if d == 0:
                yacc[rows, :] = yc
            else:
                yacc[rows, :] += yc
        if write_state:
            st_ref[d, :, 0:ns] = st[0]
            st_ref[d, :, ns:2 * ns] = st[1]
    for ch in range(nch):
        rows = pl.ds(ch * crows, crows)
        y = yacc[rows, :] + d_ref[...] * u_rows(ch)
        y_ref[rows, :] = _gelu_tanh(y).astype(y_ref.dtype)


def s5_scan(u, wb, wc, a_cat, d_skip, h0, *, L, nseq, nbt, write_state):
    has_h0 = h0 is not None
    st_spec = pl.BlockSpec((2, None, None, nseq, 2 * S5_NS), lambda g, b: (0, g, b, 0, 0))
    in_specs = [pl.BlockSpec((None, L * nseq, 128), lambda g, b: (b, 0, g)),
                pl.BlockSpec((2, None, 128, 2 * S5_NS), lambda g, b: (0, g, 0, 0)),
                pl.BlockSpec((2, None, 2 * S5_NS, 128), lambda g, b: (0, g, 0, 0)),
                pl.BlockSpec((2, None, 1, 2 * S5_NS), lambda g, b: (0, g, 0, 0)),
                pl.BlockSpec((1, 128), lambda g, b: (0, g))]
    args = [u, wb, wc, a_cat, d_skip.reshape(1, D)]
    if has_h0:
        in_specs.append(st_spec)
        args.append(h0)
    out_shape = [jax.ShapeDtypeStruct((nbt, L * nseq, D), BF16)]
    out_specs = [pl.BlockSpec((None, L * nseq, 128), lambda g, b: (b, 0, g))]
    if write_state:
        out_shape.append(jax.ShapeDtypeStruct((2, S5_GB, nbt, nseq, 2 * S5_NS), F32))
        out_specs.append(st_spec)
    return pl.pallas_call(
        functools.partial(_s5_kernel, L=L, nseq=nseq, has_h0=has_h0, write_state=write_state),
        out_shape=tuple(out_shape),
        grid=(S5_GB, nbt),
        in_specs=in_specs,
        out_specs=tuple(out_specs),
        scratch_shapes=[pltpu.VMEM((S5_CHUNK_ROWS, 2 * S5_NS), F32), pltpu.VMEM((L * nseq, 128), F32)],
        compiler_params=_cp(2, 48),
        name="s5_ctx" if write_state else "s5_lat",
    )(*args)


def s5_params(lam_re, lam_im, log_dt, b_re, b_im, c_re, c_im):
    dt = jnp.exp(log_dt)[..., None]
    dl_re, dl_im = lam_re * dt, lam_im * dt
    mag = jnp.exp(dl_re)
    a_re, a_im = mag * jnp.cos(dl_im), mag * jnp.sin(dl_im)
    den = lam_re * lam_re + lam_im * lam_im
    z_re = a_re - 1.0
    coef_re = (z_re * lam_re + a_im * lam_im) / den
    coef_im = (a_im * lam_re - z_re * lam_im) / den
    bb_re = coef_re[..., None] * b_re - coef_im[..., None] * b_im
    bb_im = coef_re[..., None] * b_im + coef_im[..., None] * b_re
    eye = jnp.eye(8, dtype=F32)

    def pack_b(bb):
        t = bb.reshape(2, S5_GB, 8, 64, 16).transpose(0, 1, 2, 4, 3)
        return (t[:, :, :, :, None, :] * eye[None, None, :, None, :, None]).reshape(2, S5_GB, 128, S5_NS)

    def pack_c(cc):
        t = cc.reshape(2, S5_GB, 8, 16, 64).transpose(0, 1, 2, 4, 3)
        return (t[:, :, :, :, None, :] * eye[None, None, :, None, :, None]).reshape(2, S5_GB, S5_NS, 128)

    wb = jnp.concatenate([pack_b(bb_re), pack_b(bb_im)], axis=-1).astype(BF16)
    wc = jnp.concatenate([pack_c(c_re), -pack_c(c_im)], axis=2).astype(BF16)
    a_cat = jnp.concatenate([a_re.reshape(2, S5_GB, S5_NS), a_im.reshape(2, S5_GB, S5_NS)], axis=-1)
    return wb, wc, a_cat.reshape(2, S5_GB, 1, 2 * S5_NS)


def _rope(x, cos, sin_signed):
    lane = lax.broadcasted_iota(jnp.int32, x.shape, 1)
    first_half = (lane % 64) < 32
    rot = jnp.where(first_half, pltpu.roll(x, 96, 1), pltpu.roll(x, 32, 1))
    return x * cos + rot * sin_signed


def _attn_kernel(lam_ref, q_ref, kn_ref, vn_ref, sub_ref, *rest, n_cache, rope, lam_init):
    rest = list(rest)
    if n_cache:
        ck_ref, cv_ref = rest.pop(0), rest.pop(0)
    if rope:
        cosq, sinq, cosk, sink = (rest.pop(0) for _ in range(4))
    o_ref, kall, vall = rest
    dh = DA_DH

    @pl.when(pl.program_id(2) == 0)
    def _():
        if n_cache:
            kall[0:n_cache, :] = ck_ref[...].astype(BF16)
            vall[0:n_cache, :] = cv_ref[...].astype(BF16)
        kn = kn_ref[...]
        if rope:
            kn = jnp.concatenate([_rope(kn[:, m * dh:(m + 1) * dh], cosk[...], sink[...]) for m in (0, 1)],
                                 axis=1)
        kall[n_cache:, :] = kn.astype(BF16)
        vall[n_cache:, :] = vn_ref[...].astype(BF16)

    q = q_ref[...]
    nt = (((1,), (1,)), ((), ()))
    probs = []
    for m in (0, 1):
        qm = q[:, m * dh:(m + 1) * dh]
        if rope:
            qm = _rope(qm, cosq[...], sinq[...])
        s = lax.dot_general(qm.astype(BF16), kall[:, m * dh:(m + 1) * dh], nt,
                            preferred_element_type=F32) * (dh ** -0.5)
        e = jnp.exp(s - jnp.max(s, axis=-1, keepdims=True))
        probs.append(e / jnp.sum(e, axis=-1, keepdims=True))
    p = (probs[0] - lam_ref[0] * probs[1]).astype(BF16)
    o = jnp.dot(p, vall[...], preferred_element_type=F32)
    of = o * lax.rsqrt(jnp.mean(o * o, axis=-1, keepdims=True) + EPS) * (1.0 - lam_init)
    o_ref[...] = (of * sub_ref[...]).astype(o_ref.dtype)


def diff_attention(qkv, subln, lam, lam_init, cache_k, cache_v, rope_tabs, *, ctx):
    tq = 256
    nb, L, blk0 = (CTX_B, CTX_L, 0) if ctx else (LAT_B, LAT_L, N_CTX // LAT_L)
    nq = L // tq
    qblk0 = blk0 * (LAT_L // tq) if not ctx else 0
    n_cache = 0 if ctx else PAST
    in_specs = [pl.BlockSpec(memory_space=pltpu.SMEM),
                pl.BlockSpec((tq, DA_VD), lambda b, h, qi: (qblk0 + b * nq + qi, h)),
                pl.BlockSpec((L, DA_VD), lambda b, h, qi: (blk0 + b, DA_H + h)),
                pl.BlockSpec((L, DA_VD), lambda b, h, qi: (blk0 + b, 2 * DA_H + h)),
                pl.BlockSpec((1, DA_VD), lambda b, h, qi: (0, h))]
    args = [lam, qkv, qkv, qkv, subln.reshape(1, D)]
    if not ctx:
        cache_spec = pl.BlockSpec((None, PAST, DA_VD), lambda b, h, qi: (b, 0, h))
        in_specs += [cache_spec, cache_spec]
        args += [cache_k.reshape(LAT_B, PAST, D), cache_v.reshape(LAT_B, PAST, D)]
        cos, sin_signed = rope_tabs
        in_specs += [pl.BlockSpec((tq, DA_DH), lambda b, h, qi: (qi, 0))] * 2
        in_specs += [pl.BlockSpec((L, DA_DH), lambda b, h, qi: (0, 0))] * 2
        args += [cos, sin_signed, cos, sin_signed]
    lk = n_cache + L
    return pl.pallas_call(
        functools.partial(_attn_kernel, n_cache=n_cache, rope=not ctx, lam_init=lam_init),
        out_shape=jax.ShapeDtypeStruct((nb * L, D), BF16),
        grid=(nb, DA_H, nq),
        in_specs=in_specs,
        out_specs=pl.BlockSpec((tq, DA_VD), lambda b, h, qi: (b * nq + qi, h)),
        scratch_shapes=[pltpu.VMEM((lk, 2 * DA_DH), BF16), pltpu.VMEM((lk, DA_VD), BF16)],
        compiler_params=_cp(3, 40),
        name="diff_attn_ctx" if ctx else "diff_attn_lat",
    )(*args)


def rope_tables(L):
    rows = L // GRID_W
    row = jnp.repeat(jnp.arange(rows, dtype=F32), GRID_W)
    col = jnp.tile(jnp.arange(GRID_W, dtype=F32), rows)
    n_freq = DA_DH // 4
    inv = ROPE_BASE ** (-jnp.arange(n_freq, dtype=F32) / n_freq)
    ang_r, ang_c = row[:, None] * inv, col[:, None] * inv
    cos = jnp.concatenate([jnp.cos(ang_r)] * 2 + [jnp.cos(ang_c)] * 2, axis=-1)
    sin_signed = jnp.concatenate([-jnp.sin(ang_r), jnp.sin(ang_r), -jnp.sin(ang_c), jnp.sin(ang_c)], axis=-1)
    return cos, sin_signed


def _gather_rows_kernel(tok_ref, nu_ref, h_hbm, o_ref, buf, sem, *, rb):
    i = pl.program_id(0)
    n_steps_used = nu_ref[0] * (MOE_TM // rb)

    def issue(step, slot):
        base = step * rb

        def body(r, carry):
            pltpu.make_async_copy(h_hbm.at[pl.ds(tok_ref[base + r] * PITCH, SLAB), :],
                                  buf.at[slot, pl.ds(r * PITCH, SLAB), :],
                                  sem.at[slot]).start()
            return carry

        lax.fori_loop(0, rb, body, 0)

    @pl.when(i == 0)
    def _():
        issue(0, 0)

    @pl.when(i + 1 < n_steps_used)
    def _():
        issue(i + 1, (i + 1) % 2)

    @pl.when(i < n_steps_used)
    def _():
        slot = i % 2
        pltpu.make_async_copy(h_hbm.at[pl.ds(0, rb * SLAB), :], buf.at[slot, pl.ds(0, rb * SLAB), :],
                              sem.at[slot]).wait()
        for c in range(SLAB):
            o_ref[:, c * 128:(c + 1) * 128] = buf[slot, pl.ds(c, rb, stride=PITCH), :].astype(o_ref.dtype)

    @pl.when(i >= n_steps_used)
    def _():
        o_ref[...] = jnp.zeros(o_ref.shape, o_ref.dtype)


def gather_rows(h_slabs, row_tok, n_used):
    rb = GATHER_RB
    return pl.pallas_call(
        functools.partial(_gather_rows_kernel, rb=rb),
        out_shape=jax.ShapeDtypeStruct((MOE_ROWS, D), BF16),
        grid_spec=pltpu.PrefetchScalarGridSpec(
            num_scalar_prefetch=2,
            grid=(MOE_ROWS // rb,),
            in_specs=[pl.BlockSpec(memory_space=pl.ANY)],
            out_specs=pl.BlockSpec((rb, D), lambda i, tok, nu: (i, 0)),
            scratch_shapes=[pltpu.VMEM((2, rb * PITCH, 128), F32), pltpu.SemaphoreType.DMA((2,))]),
        compiler_params=_cp(1, 40),
        name="moe_gather",
    )(row_tok, n_used, h_slabs)


def _ffn_up_kernel(be_ref, first_ref, nu_ref, x_ref, wg_ref, wu_ref, a_ref, wbf, *, tf):
    i = pl.program_id(1)

    @pl.when(first_ref[i] == 1)
    def _():
        wbf[:, 0:tf] = wg_ref[...].astype(BF16)
        wbf[:, tf:2 * tf] = wu_ref[...].astype(BF16)

    @pl.when(i < nu_ref[0])
    def _():
        gu = jnp.dot(x_ref[...], wbf[...], preferred_element_type=F32)
        g, u = gu[:, 0:tf], gu[:, tf:2 * tf]
        a_ref[...] = ((g * jax.nn.sigmoid(g)) * u).astype(a_ref.dtype)

    @pl.when(i >= nu_ref[0])
    def _():
        a_ref[...] = jnp.zeros(a_ref.shape, a_ref.dtype)


def _ffn_down_kernel(be_ref, first_ref, nu_ref, a_ref, w2_ref, y_ref, wbf):
    i = pl.program_id(1)

    @pl.when(first_ref[i] == 1)
    def _():
        wbf[...] = w2_ref[...].astype(BF16)

    @pl.when(i < nu_ref[0])
    def _():
        y_ref[...] = jnp.dot(a_ref[...], wbf[...], preferred_element_type=F32)

    @pl.when(i >= nu_ref[0])
    def _():
        y_ref[...] = jnp.zeros(y_ref.shape, y_ref.dtype)


def expert_ffn(xs, w13_all, w2_all, layer, blk_e, blk_first, n_used):
    tm, tf, tn = MOE_TM, MOE_TF, MOE_TN
    nj = D_EXP // tf

    def rows(i, nu):
        return jnp.minimum(i, nu[0] - 1)

    act = pl.pallas_call(
        functools.partial(_ffn_up_kernel, tf=tf),
        out_shape=jax.ShapeDtypeStruct((MOE_ROWS, D_EXP), BF16),
        grid_spec=pltpu.PrefetchScalarGridSpec(
            num_scalar_prefetch=3,
            grid=(nj, MOE_NB),
            in_specs=[pl.BlockSpec((tm, D), lambda j, i, be, fi, nu: (rows(i, nu), 0)),
                      pl.BlockSpec((None, None, D, tf), lambda j, i, be, fi, nu: (layer, be[i], 0, j)),
                      pl.BlockSpec((None, None, D, tf), lambda j, i, be, fi, nu: (layer, be[i], 0, nj + j))],
            out_specs=pl.BlockSpec((tm, tf), lambda j, i, be, fi, nu: (i, j)),
            scratch_shapes=[pltpu.VMEM((D, 2 * tf), BF16)]),
        compiler_params=_cp(2, 44),
        name="moe_ffn_up",
    )(blk_e, blk_first, n_used, xs, w13_all, w13_all)
    return pl.pallas_call(
        _ffn_down_kernel,
        out_shape=jax.ShapeDtypeStruct((MOE_ROWS, D), F32),
        grid_spec=pltpu.PrefetchScalarGridSpec(
            num_scalar_prefetch=3,
            grid=(D // tn, MOE_NB),
            in_specs=[pl.BlockSpec((tm, D_EXP), lambda n, i, be, fi, nu: (rows(i, nu), 0)),
                      pl.BlockSpec((None, None, D_EXP, tn), lambda n, i, be, fi, nu: (layer, be[i], 0, n))],
            out_specs=pl.BlockSpec((tm, tn), lambda n, i, be, fi, nu: (i, n)),
            scratch_shapes=[pltpu.VMEM((D_EXP, tn), BF16)]),
        compiler_params=_cp(2, 44),
        name="moe_ffn_down",
    )(blk_e, blk_first, n_used, act, w2_all)


def _combine_kernel(pos_ref, x_ref, gt_ref, gate_ref, fw_ref, ys_hbm, o_ref, buf, sem, *, tc, nsteps, final):
    i = pl.program_id(0)

    def issue(step, slot):
        base = step * tc

        def body(r, carry):
            for kk in (0, 1):
                p = pos_ref[kk * N_TOK + base + r]
                pltpu.make_async_copy(ys_hbm.at[pl.ds(p, 1), :], buf.at[slot, kk, pl.ds(r, 1), :],
                                      sem.at[slot]).start()
            return carry

        lax.fori_loop(0, tc, body, 0)

    @pl.when(i == 0)
    def _():
        issue(0, 0)

    @pl.when(i + 1 < nsteps)
    def _():
        issue(i + 1, (i + 1) % 2)

    slot = i % 2
    for kk in (0, 1):
        pltpu.make_async_copy(ys_hbm.at[pl.ds(0, tc), :], buf.at[slot, kk], sem.at[slot]).wait()
    gt = gt_ref[...]
    moe = gt[:, 0:1] * buf[slot, 0] + gt[:, 1:2] * buf[slot, 1]
    x2 = x_ref[...] + gate_ref[...] * moe
    if final:
        x2 = x2 * lax.rsqrt(jnp.mean(x2 * x2, axis=-1, keepdims=True) + EPS) * fw_ref[...]
    o_ref[...] = x2


def moe_combine(x, ys, pos, gates_t, mod, final_w, *, final):
    tc = COMBINE_TC
    nsteps = N_TOK // tc
    return pl.pallas_call(
        functools.partial(_combine_kernel, tc=tc, nsteps=nsteps, final=final),
        out_shape=jax.ShapeDtypeStruct((N_TOK, D), F32),
        grid_spec=pltpu.PrefetchScalarGridSpec(
            num_scalar_prefetch=1,
            grid=(nsteps,),
            in_specs=[pl.BlockSpec((tc, D), lambda i, pos: (i, 0)),
                      pl.BlockSpec((tc, 2), lambda i, pos: (i, 0)),
                      _mod_spec(tc, J_GATE2),
                      pl.BlockSpec((1, D), lambda i, pos: (0, 0)),
                      pl.BlockSpec(memory_space=pl.ANY)],
            out_specs=pl.BlockSpec((tc, D), lambda i, pos: (i, 0)),
            scratch_shapes=[pltpu.VMEM((2, 2, tc, D), F32), pltpu.SemaphoreType.DMA((2,))]),
        compiler_params=_cp(1, 40),
        name="moe_combine",
    )(pos, x, gates_t, mod, final_w.reshape(1, D), ys)


def dispatch_plan(ids, rank, tile_cnt):
    tile_off = jnp.cumsum(tile_cnt, axis=0) - tile_cnt
    counts = jnp.sum(tile_cnt, axis=0)
    nblk = (counts + MOE_TM - 1) // MOE_TM
    blk_end = jnp.cumsum(nblk)
    n_used = blk_end[-1]
    offs = (blk_end - nblk)[None, :] * MOE_TM + tile_off
    offs_tok = jnp.repeat(offs, ROUTE_TM, axis=0)
    pick = ids[:, :, None] == jnp.arange(N_EXP, dtype=jnp.int32)[None, None, :]
    pos = (jnp.sum(jnp.where(pick, offs_tok[None], 0), axis=-1) + rank).astype(jnp.int32)
    tok = jnp.tile(jnp.arange(N_TOK, dtype=jnp.int32), 2)
    row_tok = (jnp.arange(MOE_ROWS, dtype=jnp.int32) % N_TOK).at[pos.reshape(-1)].set(tok)
    blk = jnp.minimum(jnp.arange(MOE_NB, dtype=jnp.int32), n_used - 1)
    blk_e = jnp.minimum(jnp.searchsorted(blk_end, blk, side="right"), N_EXP - 1).astype(jnp.int32)
    blk_first = jnp.concatenate([jnp.ones((1,), jnp.int32), (blk_e[1:] != blk_e[:-1]).astype(jnp.int32)])
    return pos.reshape(-1), row_tok, blk_e, blk_first, n_used.reshape(1).astype(jnp.int32)


def moe_layer(x, norm_g, mod, router_wt, router_b, w13_all, w2_all, layer, final_w, *, final):
    h2, ids, gates, rank, tile_cnt = normmod_route(x, norm_g, mod, router_wt, router_b)
    pos, row_tok, blk_e, blk_first, n_used = dispatch_plan(ids, rank, tile_cnt[:, :, 0])
    xs = gather_rows(h2, row_tok, n_used)
    ys = expert_ffn(xs, w13_all, w2_all, layer, blk_e, blk_first, n_used)
    return moe_combine(x, ys, pos, gates.T, mod, final_w, final=final)


def _to_time_major(h):
    hc = h[:N_CTX].reshape(CTX_B // 8, 8, CTX_L, D).transpose(0, 2, 1, 3).reshape(CTX_B // 8, CTX_L * 8, D)
    hl = h[N_CTX:].reshape(LAT_B, LAT_L, D).transpose(1, 0, 2)
    hl = jnp.pad(hl, ((0, 0), (0, 8 - LAT_B), (0, 0))).reshape(1, LAT_L * 8, D)
    return hc, hl


def _from_time_major(yc, yl):
    yc = yc.reshape(CTX_B // 8, CTX_L, 8, D).transpose(0, 2, 1, 3).reshape(N_CTX, D)
    yl = yl.reshape(LAT_L, 8, D)[:, :LAT_B].transpose(1, 0, 2).reshape(N_LAT, D)
    return jnp.concatenate([yc, yl], axis=0)


def kernel(x_prompt, x_sample, state_ret_l0, state_s5_l1, cache_k_l2, cache_v_l2, state_ret_l3, c, c_ctx, ada_w, ada_b, norm_w, final_norm_w, router_w, router_b, moe_w13, moe_w2, l0_ret_w_in, l0_ret_w_o, l0_ret_gn, l0_ret_decay, l1_s5_lam_re, l1_s5_lam_im, l1_s5_log_dt, l1_s5_b_re, l1_s5_b_im, l1_s5_c_re, l1_s5_c_im, l1_s5_d, l1_s5_w_glu, l1_s5_b_glu, l2_da_w_qkv, l2_da_w_o, l2_da_subln, l2_da_lambda, l3_ret_w_in, l3_ret_w_o, l3_ret_gn, l3_ret_decay):
    x = jnp.concatenate([x_prompt.reshape(N_CTX, D), x_sample.reshape(N_LAT, D)], axis=0)
    cvecs = jnp.concatenate([c_ctx[None], c, jnp.zeros((MOD_ROWS - 1 - LAT_B, D), F32)], axis=0)
    mods = modulation_all(cvecs, ada_w, ada_b)
    router_wt = router_w.T

    def retention_layer(i, x, w_in, w_o, gn, decay, s0, proj_dtype):
        mod = mods[i]
        h = normmod(x, norm_w[i, 0], mod, J_SHIFT1, J_SCALE1, BF16)
        qkvg = matmul(h, w_in, proj_dtype)
        log_gamma = -jnp.exp(decay)
        o_c, st = retention(qkvg, gn, log_gamma, None, ctx=True)
        (o_l,) = retention(qkvg, gn, log_gamma, s0, ctx=False)
        return matmul_residual(o_c, o_l, w_o.astype(BF16), x, mod, J_GATE1, tm=512, tn=256), st

    def moe(i, x):
        return moe_layer(x, norm_w[i, 1], mods[i], router_wt, router_b, moe_w13, moe_w2, i, final_norm_w,
                         final=(i == DEPTH - 1))

    x, st_ret0 = retention_layer(0, x, l0_ret_w_in, l0_ret_w_o, l0_ret_gn, l0_ret_decay, state_ret_l0, F32)
    x = moe(0, x)

    mod = mods[1]
    h = normmod(x, norm_w[1, 0], mod, J_SHIFT1, J_SCALE1, F32)
    wb, wc, a_cat = s5_params(l1_s5_lam_re, l1_s5_lam_im, l1_s5_log_dt, l1_s5_b_re, l1_s5_b_im,
                              l1_s5_c_re, l1_s5_c_im)
    u_c, u_l = _to_time_major(h)
    h0 = state_s5_l1.reshape(LAT_B, 2, S5_GB, 8, 64, 2).transpose(1, 2, 0, 5, 3, 4)
    h0 = jnp.pad(h0.reshape(2, S5_GB, LAT_B, 2 * S5_NS), ((0, 0), (0, 0), (0, 8 - LAT_B), (0, 0)))
    h0 = h0.reshape(2, S5_GB, 1, 8, 2 * S5_NS)
    y_c, st5 = s5_scan(u_c, wb, wc, a_cat, l1_s5_d, None, L=CTX_L, nseq=8, nbt=CTX_B // 8, write_state=True)
    (y_l,) = s5_scan(u_l, wb, wc, a_cat, l1_s5_d, h0, L=LAT_L, nseq=8, nbt=1, write_state=False)
    y = _from_time_major(y_c, y_l)
    x = matmul_glu_residual(y, l1_s5_w_glu, l1_s5_b_glu, x, mod, J_GATE1)
    st_s5 = st5.reshape(2, S5_GB, CTX_B // 8, 8, 2, 8, 64).transpose(2, 3, 0, 1, 5, 6, 4)
    st_s5 = st_s5.reshape(CTX_B, 2, S5_GB * 8, 64, 2)
    x = moe(1, x)

    mod = mods[2]
    lam_init = 0.8 - 0.6 * math.exp(-0.3 * 2)
    lp = l2_da_lambda
    lam = (jnp.exp(jnp.sum(lp[0] * lp[1])) - jnp.exp(jnp.sum(lp[2] * lp[3])) + lam_init).reshape(1)
    h = normmod(x, norm_w[2, 0], mod, J_SHIFT1, J_SCALE1, BF16)
    qkv = matmul(h, l2_da_w_qkv, F32)
    a_c = diff_attention(qkv, l2_da_subln, lam, lam_init, None, None, None, ctx=True)
    a_l = diff_attention(qkv, l2_da_subln, lam, lam_init, cache_k_l2, cache_v_l2, rope_tables(LAT_L), ctx=False)
    x = matmul_residual(a_c, a_l, l2_da_w_o.astype(BF16), x, mod, J_GATE1)
    new_k = qkv[:N_CTX, D:2 * D].reshape(CTX_B, CTX_L, DA_H, 2, DA_DH)
    new_v = qkv[:N_CTX, 2 * D:].reshape(CTX_B, CTX_L, DA_H, DA_VD)
    x = moe(2, x)

    x, st_ret3 = retention_layer(3, x, l3_ret_w_in, l3_ret_w_o, l3_ret_gn, l3_ret_decay, state_ret_l3, BF16)
    y = moe(3, x)

    y_prompt = y[:N_CTX].reshape(CTX_B, CTX_L, D)
    y_sample = y[N_CTX:].reshape(LAT_B, LAT_L, D)
    return (y_prompt, y_sample, st_ret0, st_s5, new_k, new_v, st_ret3)
```
